```python
import jax, jax.numpy as jnp
from jax import lax
import numpy as np

D_MODEL = 1024
BATCH = 2
SEQ = 8192
DEPTH = 2
DEC_BATCH = 2
DEC_SEQ = 16384
PAST_LEN = 128

PLE_DIM = 256
N_MIXERS = 2
N_A = (DEPTH + 1) // 2
N_B = DEPTH // 2
M_INNER = 2 * D_MODEL
M_HEADS = 4
M_HEAD_DIM = M_INNER // M_HEADS
M_QKV_BLOCK = 4
M_NBLK = M_INNER // M_QKV_BLOCK
M_CONV = 5
M_CHUNK = 128
H_EXPAND = 128
H_HEADS = D_MODEL // H_EXPAND
H_KDIM = H_EXPAND
H_VDIM = D_MODEL // H_HEADS
H_CHUNK = 64
D_FF = 2816
EPS = 1e-6

kernel_name = "bidir_mlstm_hgrn2_macaron_encoder"


def rmsnorm(x, g):
    xf = x.astype(jnp.float32)
    y = xf * lax.rsqrt(jnp.mean(xf * xf, axis=-1, keepdims=True) + EPS)
    return (y * g.astype(jnp.float32)).astype(x.dtype)


def head_rmsnorm(h, g):
    h = h * lax.rsqrt(jnp.mean(h * h, axis=-1, keepdims=True) + EPS)
    return h.reshape(h.shape[0], h.shape[1], -1) * g.astype(jnp.float32)


def swiglu(x, w_in, w_out):
    a, u = jnp.split(x @ w_in, 2, axis=-1)
    return (jax.nn.silu(a) * u) @ w_out


def centred_dwconv(x, w, b):
    y = lax.conv_general_dilated(x, w[:, None, :], window_strides=(1,),
                                 padding=[(M_CONV // 2, M_CONV // 2)],
                                 dimension_numbers=('NWC', 'WIO', 'NWC'),
                                 feature_group_count=x.shape[-1])
    return y + b


def _to_chunks(t, L):
    B, S = t.shape[0], t.shape[1]
    t = t.reshape(B, S // L, L, *t.shape[2:])
    return jnp.moveaxis(jnp.moveaxis(t, 1, 0), 2, 3)


def _from_chunks(t):
    t = jnp.moveaxis(jnp.moveaxis(t, 3, 2), 0, 1)
    return t.reshape(t.shape[0], t.shape[1] * t.shape[2], *t.shape[3:])


def _flip(t):
    return jnp.flip(t, axis=1)


def mlstm_scan(q, k, v, li, lf):
    B, S, H, d = q.shape
    L = M_CHUNK
    mask = jnp.tril(jnp.ones((L, L), dtype=bool))

    def step(carry, xs):
        C, n, m = carry
        qb, kb, vb, ib, fb = xs
        b = jnp.cumsum(fb, axis=-1)
        dlog = jnp.where(mask, b[..., :, None] - b[..., None, :] + ib[..., None, :], -jnp.inf)
        m_t = jnp.maximum(b + m[..., None], jnp.max(dlog, axis=-1))
        s = jnp.einsum('bhtd,bhsd->bhts', qb, kb) * jnp.exp(dlog - m_t[..., None])
        w_inter = jnp.exp(b + m[..., None] - m_t)
        num = jnp.einsum('bhts,bhsv->bhtv', s, vb) + w_inter[..., None] * jnp.einsum('bhtd,bhdv->bhtv', qb, C)
        den = jnp.sum(s, axis=-1) + w_inter * jnp.einsum('bhtd,bhd->bht', qb, n)
        h = num / jnp.maximum(jnp.abs(den), jnp.exp(-m_t))[..., None]
        b_last = b[..., -1]
        m_new = jnp.maximum(b_last + m, jnp.max(b_last[..., None] - b + ib, axis=-1))
        wk = jnp.exp(b_last[..., None] - b + ib - m_new[..., None])
        w_old = jnp.exp(b_last + m - m_new)
        C = w_old[..., None, None] * C + jnp.einsum('bhs,bhsd,bhsv->bhdv', wk, kb, vb)
        n = w_old[..., None] * n + jnp.einsum('bhs,bhsd->bhd', wk, kb)
        return (C, n, m_new), h

    init = (jnp.zeros((B, H, d, d), jnp.float32), jnp.zeros((B, H, d), jnp.float32),
            jnp.zeros((B, H), jnp.float32))
    xs = (_to_chunks(q, L), _to_chunks(k, L), _to_chunks(v, L), _to_chunks(li, L), _to_chunks(lf, L))
    _, h = lax.scan(step, init, xs)
    return _from_chunks(h)


def hgrn2_scan(q, k, lf, v):
    B, S, H, dk = q.shape
    dv = v.shape[-1]
    L = H_CHUNK
    mask = jnp.tril(jnp.ones((L, L), dtype=bool))[:, :, None]

    def step(St, xs):
        qb, kb, fb, vb = xs
        b = jnp.cumsum(fb, axis=2)
        decay = jnp.exp(jnp.where(mask, b[:, :, :, None, :] - b[:, :, None, :, :], -jnp.inf))
        a = jnp.einsum('bhtc,bhsc,bhtsc->bhts', qb, kb, decay)
        o = jnp.einsum('bhts,bhsv->bhtv', a, vb) + jnp.einsum('bhtc,bhcv->bhtv', qb * jnp.exp(b), St)
        b_last = b[:, :, -1:, :]
        St = jnp.exp(b_last[:, :, 0, :])[..., None] * St + \
            jnp.einsum('bhsc,bhsv->bhcv', kb * jnp.exp(b_last - b), vb)
        return St, o

    S0 = jnp.zeros((B, H, dk, dv), jnp.float32)
    xs = (_to_chunks(q, L), _to_chunks(k, L), _to_chunks(lf, L), _to_chunks(v, L))
    _, o = lax.scan(step, S0, xs)
    return _from_chunks(o)


def mlstm_mixer(xn, w_in, conv_w, conv_b, wq, wk, wv, w_gate, b_gate, head_norm, skip, w_out):
    B, S, _ = xn.shape
    xm, z = jnp.split(xn @ w_in, 2, axis=-1)
    xc = jax.nn.silu(centred_dwconv(xm, conv_w, conv_b))

    def blockdiag(t, w):
        return jnp.einsum('bsnc,ncd->bsnd', t.reshape(B, S, M_NBLK, M_QKV_BLOCK), w).reshape(B, S, M_INNER)

    q, k, v = blockdiag(xc, wq), blockdiag(xc, wk), blockdiag(xm, wv)
    gates = (jnp.concatenate([q, k, v], axis=-1) @ w_gate + b_gate).astype(jnp.float32)
    gates = gates.reshape(B, S, 4, M_HEADS)
    heads = lambda t: t.reshape(B, S, M_HEADS, M_HEAD_DIM).astype(jnp.float32)
    qh, kh, vh = heads(q), heads(k) * (M_HEAD_DIM ** -0.5), heads(v)
    li_f, lf_f = gates[:, :, 0], jax.nn.log_sigmoid(gates[:, :, 1])
    li_b, lf_b = gates[:, :, 2], jax.nn.log_sigmoid(gates[:, :, 3])
    h = mlstm_scan(qh, kh, vh, li_f, lf_f) + \
        _flip(mlstm_scan(_flip(qh), _flip(kh), _flip(vh), _flip(li_b), _flip(lf_b)))
    h = head_rmsnorm(h, head_norm).astype(xc.dtype) + skip * xc
    return (h * jax.nn.silu(z)) @ w_out


def lower_bounds(logits):
    pr = jax.nn.softmax(logits.astype(jnp.float32), axis=0)
    return jnp.cumsum(pr, axis=0) - pr[0]


def hgrn2_mixer(xn, w_in, b_f, lb_f, lb_b, head_norm, w_out):
    B, S, _ = xn.shape
    q, a_f, a_b, v, g = jnp.split(xn @ w_in, 5, axis=-1)
    heads = lambda t, d: t.reshape(B, S, H_HEADS, d).astype(jnp.float32)

    def forget(a, lb):
        a = a.astype(jnp.float32)
        lf = jnp.logaddexp(jnp.log(lb), jnp.log1p(-lb) + jax.nn.log_sigmoid(a))
        k = (1.0 - lb) * jax.nn.sigmoid(-a)
        return heads(k, H_KDIM), heads(lf, H_KDIM)

    k_f, lf_f = forget(a_f + b_f[0], lb_f)
    k_b, lf_b = forget(a_b + b_f[1], lb_b)
    qh, vh = heads(q, H_KDIM), heads(v, H_VDIM)
    o = hgrn2_scan(qh, k_f, lf_f, vh) + \
        _flip(hgrn2_scan(_flip(qh), _flip(k_b), _flip(lf_b), _flip(vh)))
    o = head_rmsnorm(o, head_norm).astype(xn.dtype) * jax.nn.sigmoid(g)
    return o @ w_out


def trunk(x, p, ffn1_norm, ffn1_w_in, ffn1_w_out, mix_norm,
          m_w_in, m_conv_w, m_conv_b, m_wq, m_wk, m_wv, m_w_gate, m_b_gate, m_head_norm, m_skip, m_w_out,
          h_w_in, h_b_f, h_lb_logits, h_head_norm, h_w_out,
          ffn2_norm, ffn2_w_in, ffn2_w_out, ple_norm, ple_w_gate, ple_w_proj, final_norm):
    lb_fwd = lower_bounds(h_lb_logits[0])
    lb_bwd = lower_bounds(h_lb_logits[1])
    for i in range(DEPTH):
        x = x + 0.5 * swiglu(rmsnorm(x, ffn1_norm[i]), ffn1_w_in[i], ffn1_w_out[i])
        xn = rmsnorm(x, mix_norm[i])
        j = i // N_MIXERS
        if i % N_MIXERS == 0:
            x = x + mlstm_mixer(xn, m_w_in[j], m_conv_w[j], m_conv_b[j], m_wq[j], m_wk[j], m_wv[j],
                                m_w_gate[j], m_b_gate[j], m_head_norm[j], m_skip[j], m_w_out[j])
        else:
            x = x + hgrn2_mixer(xn, h_w_in[j], h_b_f[j], lb_fwd[i], lb_bwd[i], h_head_norm[j], h_w_out[j])
        x = x + 0.5 * swiglu(rmsnorm(x, ffn2_norm[i]), ffn2_w_in[i], ffn2_w_out[i])
        x = x + jax.nn.sigmoid(rmsnorm(x, ple_norm[i]) @ ple_w_gate[i]) * (p[i] @ ple_w_proj[i])
    return rmsnorm(x, final_norm)


def setup_inputs(seed: int = 0) -> dict:
    key = jax.random.key(seed)
    ks = iter(jax.random.split(key, 48))
    nrm = lambda shape, scale: jax.random.normal(next(ks), shape, jnp.float32) * scale
    gain = lambda shape: 1.0 + nrm(shape, 0.02)
    D = D_MODEL
    ib = lambda: nrm((N_A, M_HEADS), 0.1)
    fb = lambda: jnp.broadcast_to(jnp.linspace(3.0, 6.0, M_HEADS, dtype=jnp.float32), (N_A, M_HEADS)) + nrm((N_A, M_HEADS), 0.1)
    return {
        "x_prompt": nrm((BATCH, SEQ, D), 1.0),
        "x_sample": nrm((DEC_BATCH, DEC_SEQ, D), 1.0),
        "p_prompt": nrm((DEPTH, BATCH, SEQ, PLE_DIM), 1.0),
        "p_sample": nrm((DEPTH, DEC_BATCH, DEC_SEQ, PLE_DIM), 1.0),
        "ffn1_norm": gain((DEPTH, D)),
        "ffn1_w_in": nrm((DEPTH, D, 2 * D_FF), D ** -0.5),
        "ffn1_w_out": nrm((DEPTH, D_FF, D), D_FF ** -0.5),
        "mix_norm": gain((DEPTH, D)),
        "m_w_in": nrm((N_A, D, 2 * M_INNER), D ** -0.5),
        "m_conv_w": nrm((N_A, M_CONV, M_INNER), M_CONV ** -0.5),
        "m_conv_b": nrm((N_A, M_INNER), 0.02),
        "m_wq": nrm((N_A, M_NBLK, M_QKV_BLOCK, M_QKV_BLOCK), M_QKV_BLOCK ** -0.5),
        "m_wk": nrm((N_A, M_NBLK, M_QKV_BLOCK, M_QKV_BLOCK), M_QKV_BLOCK ** -0.5),
        "m_wv": nrm((N_A, M_NBLK, M_QKV_BLOCK, M_QKV_BLOCK), M_QKV_BLOCK ** -0.5),
        "m_w_gate": nrm((N_A, 3 * M_INNER, 4 * M_HEADS), (3 * M_INNER) ** -0.5),
        "m_b_gate": jnp.concatenate([ib(), fb(), ib(), fb()], axis=-1),
        "m_head_norm": gain((N_A, M_INNER)),
        "m_skip": gain((N_A, M_INNER)),
        "m_w_out": nrm((N_A, M_INNER, D), M_INNER ** -0.5),
        "h_w_in": nrm((N_B, D, 5 * D), D ** -0.5),
        "h_b_f": nrm((N_B, 2, D), 0.1),
        "h_lb_logits": nrm((2, DEPTH, D), 0.5),
        "h_head_norm": gain((N_B, D)),
        "h_w_out": nrm((N_B, D, D), D ** -0.5),
        "ffn2_norm": gain((DEPTH, D)),
        "ffn2_w_in": nrm((DEPTH, D, 2 * D_FF), D ** -0.5),
        "ffn2_w_out": nrm((DEPTH, D_FF, D), D_FF ** -0.5),
        "ple_norm": gain((DEPTH, D)),
        "ple_w_gate": nrm((DEPTH, D, D), D ** -0.5),
        "ple_w_proj": nrm((DEPTH, PLE_DIM, D), PLE_DIM ** -0.5),
        "final_norm": gain((D,)),
    }


def reference(x_prompt, x_sample, p_prompt, p_sample, ffn1_norm, ffn1_w_in, ffn1_w_out, mix_norm,
              m_w_in, m_conv_w, m_conv_b, m_wq, m_wk, m_wv, m_w_gate, m_b_gate, m_head_norm, m_skip, m_w_out,
              h_w_in, h_b_f, h_lb_logits, h_head_norm, h_w_out,
              ffn2_norm, ffn2_w_in, ffn2_w_out, ple_norm, ple_w_gate, ple_w_proj, final_norm):
    weights = (ffn1_norm, ffn1_w_in, ffn1_w_out, mix_norm,
               m_w_in, m_conv_w, m_conv_b, m_wq, m_wk, m_wv, m_w_gate, m_b_gate, m_head_norm, m_skip, m_w_out,
               h_w_in, h_b_f, h_lb_logits, h_head_norm, h_w_out,
               ffn2_norm, ffn2_w_in, ffn2_w_out, ple_norm, ple_w_gate, ple_w_proj, final_norm)
    y_prompt = trunk(x_prompt, p_prompt, *weights)
    y_sample = trunk(x_sample, p_sample, *weights)
    return (y_prompt, y_sample)
```

```python
import functools

import jax
import jax.numpy as jnp
from jax import lax
from jax.experimental import pallas as pl
from jax.experimental.pallas import tpu as pltpu

F32 = jnp.float32
BF16 = jnp.bfloat16
EPS = 1e-6

M_HEADS = 4
M_QKV_BLOCK = 4
M_CONV = 5
H_HEADS = 8
PLE_GROUP = 256
TOKEN_TILE = 512
SCAN_CHUNK = 256
CONV_HALO = 16
GATE_LANES = 128
VMEM_LIMIT = 56 * 1024 * 1024


def _params(n_axes):
    return pltpu.CompilerParams(dimension_semantics=("arbitrary",) * n_axes,
                                vmem_limit_bytes=VMEM_LIMIT)


def _const_spec(shape):
    nd = len(shape)
    return pl.BlockSpec(shape, lambda *_: (0,) * nd, pipeline_mode=pl.Buffered(1))


def _rms(x, g):
    return x * lax.rsqrt(jnp.mean(x * x, axis=-1, keepdims=True) + EPS) * g


def _sigmoid(x):
    return 1.0 / (1.0 + jnp.exp(-x))


def _dot(a, b):
    return jnp.dot(a, b, preferred_element_type=F32)


def _dot_nt(a, b):
    return lax.dot_general(a, b, (((1,), (1,)), ((), ())), preferred_element_type=F32)


def _dot_tn(a, b):
    return lax.dot_general(a, b, (((0,), (0,)), ((), ())), preferred_element_type=F32)


def _ffn_body(has_ple, has_final, *refs):
    it = iter(refs)
    x_ref, g_ref, win_ref, wout_ref = (next(it) for _ in range(4))
    if has_ple:
        p_ref, pg_ref, pwg_ref, pwp_ref = (next(it) for _ in range(4))
    if has_final:
        fg_ref = next(it)
    o_ref = next(it)
    d_ff = wout_ref.shape[0]
    x = x_ref[...]
    xn = _rms(x, g_ref[...]).astype(BF16)
    a = _dot(xn, win_ref[:, :d_ff])
    u = _dot(xn, win_ref[:, d_ff:])
    act = (a * _sigmoid(a) * u).astype(BF16)
    x = x + 0.5 * _dot(act, wout_ref[...])
    if has_ple:
        xg = _rms(x, pg_ref[...]).astype(BF16)
        gate = _sigmoid(_dot(xg, pwg_ref[...]))
        x = x + gate * _dot(p_ref[...].astype(BF16), pwp_ref[...])
    if has_final:
        x = _rms(x, fg_ref[...])
    o_ref[...] = x


def _ffn(x, g, w_in, w_out, ple=None, final_g=None):
    t, d = x.shape
    tm = TOKEN_TILE
    row = lambda w: pl.BlockSpec((tm, w), lambda i: (i, 0))
    args = [x, g.reshape(1, d), w_in, w_out]
    specs = [row(d), _const_spec((1, d)), _const_spec(w_in.shape), _const_spec(w_out.shape)]
    if ple is not None:
        p, pg, pwg, pwp = ple
        args += [p, pg.reshape(1, d), pwg, pwp]
        specs += [row(p.shape[1]), _const_spec((1, d)), _const_spec(pwg.shape), _const_spec(pwp.shape)]
    if final_g is not None:
        args.append(final_g.reshape(1, d))
        specs.append(_const_spec((1, d)))
    return pl.pallas_call(
        functools.partial(_ffn_body, ple is not None, final_g is not None),
        grid=(t // tm,),
        in_specs=specs,
        out_specs=row(d),
        out_shape=jax.ShapeDtypeStruct((t, d), F32),
        compiler_params=_params(1),
        name="ffn",
    )(*args)


def _mpre_body(x_ref, g_ref, w_ref, xm_ref, z_ref):
    n = xm_ref.shape[-1]
    xn = _rms(x_ref[...], g_ref[...]).astype(BF16)
    xm_ref[...] = _dot(xn, w_ref[:, :n]).astype(BF16)
    z_ref[...] = _dot(xn, w_ref[:, n:]).astype(BF16)


def _mpre(x, g, w_in):
    t, d = x.shape
    n = w_in.shape[1] // 2
    tm = TOKEN_TILE
    row = lambda w: pl.BlockSpec((tm, w), lambda i: (i, 0))
    return pl.pallas_call(
        _mpre_body,
        grid=(t // tm,),
        in_specs=[row(d), _const_spec((1, d)), _const_spec(w_in.shape)],
        out_specs=[row(n), row(n)],
        out_shape=[jax.ShapeDtypeStruct((t, n), BF16)] * 2,
        compiler_params=_params(1),
        name="mlstm_pre",
    )(x, g.reshape(1, d), w_in)


def _log_sigmoid(x):
    return jnp.minimum(x, 0.0) - jnp.log1p(jnp.exp(-jnp.abs(x)))


def _mconv_body(k_scale, xm_ref, prev_ref, next_ref, cw_ref, cb_ref, wqk_ref, wv_ref, wg_ref, bg_ref,
                q_ref, k_ref, v_ref, xc_ref, g_ref, gt_ref, ext_ref):
    i = pl.program_id(1)
    last = pl.num_programs(1) - 1
    ts, c = xm_ref.shape[1], xm_ref.shape[2]
    hl = CONV_HALO
    pad = M_CONV // 2
    xm = xm_ref[0]
    ext_ref[0:hl, :] = jnp.where(i > 0, prev_ref[0].astype(F32), 0.0)
    ext_ref[hl:hl + ts, :] = xm.astype(F32)
    ext_ref[hl + ts:hl + ts + hl, :] = jnp.where(i < last, next_ref[0].astype(F32), 0.0)
    acc = jnp.broadcast_to(cb_ref[...], (ts, c))
    for j in range(M_CONV):
        acc = acc + cw_ref[j:j + 1, :] * ext_ref[hl - pad + j:hl - pad + j + ts, :]
    xc = (acc * _sigmoid(acc)).astype(BF16)
    xc_ref[0] = xc
    gw = PLE_GROUP
    gates = jnp.broadcast_to(bg_ref[...], (ts, GATE_LANES))
    for j in range(c // gw):
        cs = slice(j * gw, (j + 1) * gw)
        qk = _dot(xc[:, cs], wqk_ref[j])
        qj = qk[:, :gw].astype(BF16)
        kj = qk[:, gw:]
        kjb = kj.astype(BF16)
        vj = _dot(xm[:, cs], wv_ref[j]).astype(BF16)
        q_ref[0, :, cs] = qj
        k_ref[0, :, cs] = (kj * k_scale).astype(BF16)
        v_ref[0, :, cs] = vj
        gates = gates + _dot(qj, wg_ref[0, cs, :]) + _dot(kjb, wg_ref[1, cs, :]) + _dot(vj, wg_ref[2, cs, :])
    col = lax.broadcasted_iota(jnp.int32, gates.shape, 1)
    is_forget = (col % (2 * M_HEADS)) >= M_HEADS
    gates = jnp.where(is_forget, _log_sigmoid(gates), gates)
    g_ref[0] = gates
    gt_ref[0] = gates.T[:4 * M_HEADS, :]


def _mconv(xm, conv_w, conv_b, wqk, wv, wg, bg, k_scale):
    b, s, c = xm.shape
    ts = TOKEN_TILE
    hb = ts // CONV_HALO
    nh_blocks = s // CONV_HALO
    tile = lambda w, dt=None: pl.BlockSpec((1, ts, w), lambda bi, i: (bi, i, 0))
    return pl.pallas_call(
        functools.partial(_mconv_body, k_scale),
        grid=(b, s // ts),
        in_specs=[
            tile(c),
            pl.BlockSpec((1, CONV_HALO, c), lambda bi, i: (bi, jnp.maximum(i * hb - 1, 0), 0)),
            pl.BlockSpec((1, CONV_HALO, c), lambda bi, i: (bi, jnp.minimum((i + 1) * hb, nh_blocks - 1), 0)),
            _const_spec(conv_w.shape), _const_spec((1, c)),
            _const_spec(wqk.shape), _const_spec(wv.shape), _const_spec(wg.shape), _const_spec(bg.shape),
        ],
        out_specs=[tile(c), tile(c), tile(c), tile(c), tile(GATE_LANES),
                   pl.BlockSpec((1, 4 * M_HEADS, ts), lambda bi, i: (bi, 0, i))],
        out_shape=[jax.ShapeDtypeStruct((b, s, c), BF16)] * 4
        + [jax.ShapeDtypeStruct((b, s, GATE_LANES), F32), jax.ShapeDtypeStruct((b, 4 * M_HEADS, s), F32)],
        scratch_shapes=[pltpu.VMEM((ts + 2 * CONV_HALO, c), F32)],
        compiler_params=_params(2),
        name="mlstm_conv_qkv",
    )(xm, xm, xm, conv_w, conv_b.reshape(1, c), wqk, wv, wg, bg)


def _cumsum(x, axis, rev):
    n = x.shape[axis]
    idx = lax.broadcasted_iota(jnp.int32, x.shape, axis)
    s = 1
    while s < n:
        if rev:
            x = x + jnp.where(idx < n - s, pltpu.roll(x, n - s, axis), 0.0)
        else:
            x = x + jnp.where(idx >= s, pltpu.roll(x, s, axis), 0.0)
        s *= 2
    return x


def _mlstm_body(rev, final, *refs):
    if final:
        (q_ref, k_ref, v_ref, g_ref, gt_ref, hb_ref, xc_ref, z_ref, x_ref, hn_ref, sk_ref, wo_ref,
         o_ref, c_ref, n_ref, m_ref) = refs
    else:
        q_ref, k_ref, v_ref, g_ref, gt_ref, o_ref, c_ref, n_ref, m_ref = refs
    nh = M_HEADS
    ln = q_ref.shape[1]
    dh = q_ref.shape[2] // nh

    @pl.when(pl.program_id(1) == 0)
    def _():
        c_ref[...] = jnp.zeros_like(c_ref)
        n_ref[...] = jnp.zeros_like(n_ref)
        m_ref[...] = jnp.zeros_like(m_ref)

    g = g_ref[0]
    gt = gt_ref[0]
    bcol = _cumsum(g, 0, rev)
    brow = _cumsum(gt, 1, rev)
    tt = lax.broadcasted_iota(jnp.int32, (ln, ln), 0)
    ss = lax.broadcasted_iota(jnp.int32, (ln, ln), 1)
    valid = (ss >= tt) if rev else (ss <= tt)
    end = 0 if rev else ln - 1
    goff = 2 * nh if rev else 0
    if final:
        acc = jnp.zeros(o_ref.shape[1:], F32)
    for h in range(nh):
        hs = slice(h * dh, (h + 1) * dh)
        ci, cf = goff + h, goff + nh + h
        li_row, b_row = gt[ci:ci + 1, :], brow[cf:cf + 1, :]
        li_col, b_col = g[:, ci:ci + 1], bcol[:, cf:cf + 1]
        b_all = b_row[:, end:end + 1]
        m_prev = m_ref[h][0:1, 0:1]
        q, k, v = q_ref[0, :, hs], k_ref[0, :, hs], v_ref[0, :, hs]

        dlog = jnp.where(valid, b_col - b_row + li_row, -jnp.inf)
        m_t = jnp.maximum(b_col + m_prev, jnp.max(dlog, axis=-1, keepdims=True))
        s_mat = _dot_nt(q, k) * jnp.exp(dlog - m_t)
        w_inter = jnp.exp(b_col + m_prev - m_t)
        cst = c_ref[h]
        nst = n_ref[h][0:1, :]
        num = _dot(s_mat.astype(BF16), v) + w_inter * _dot(q, cst.astype(BF16))
        den = jnp.sum(s_mat, axis=-1, keepdims=True) + \
            w_inter * jnp.sum(q.astype(F32) * nst, axis=-1, keepdims=True)
        hh = num * (1.0 / jnp.maximum(jnp.abs(den), jnp.exp(-m_t)))

        a_row = b_all - b_row + li_row
        m_new = jnp.maximum(b_all + m_prev, jnp.max(a_row, axis=-1, keepdims=True))
        wk_col = jnp.exp(b_all - b_col + li_col - m_new)
        w_old = jnp.exp(b_all + m_prev - m_new)
        kw = k.astype(F32) * wk_col
        c_ref[h] = w_old * cst + _dot_tn(kw.astype(BF16), v)
        n_ref[h] = jnp.broadcast_to(w_old * nst + jnp.sum(kw, axis=0, keepdims=True), n_ref.shape[1:])
        m_ref[h] = jnp.broadcast_to(m_new, m_ref.shape[1:])

        if final:
            hsum = hh + hb_ref[0, :, hs]
            hn = hsum * lax.rsqrt(jnp.mean(hsum * hsum, axis=-1, keepdims=True) + EPS) * hn_ref[:, hs]
            y = hn + sk_ref[:, hs] * xc_ref[0, :, hs].astype(F32)
            z = z_ref[0, :, hs].astype(F32)
            y = y * (z * _sigmoid(z))
            acc = acc + _dot(y.astype(BF16), wo_ref[hs, :])
        else:
            o_ref[0, :, hs] = hh
    if final:
        o_ref[0] = x_ref[0] + acc


def _mlstm_scan(rev, q, k, v, g, gt, final=None):
    b, s, c = q.shape
    ln = SCAN_CHUNK
    nc = s // ln
    dh = c // M_HEADS
    cidx = (lambda i: nc - 1 - i) if rev else (lambda i: i)
    tile = lambda w: pl.BlockSpec((1, ln, w), lambda bi, i: (bi, cidx(i), 0))
    args = [q, k, v, g, gt]
    specs = [tile(c), tile(c), tile(c), tile(GATE_LANES),
             pl.BlockSpec((1, 4 * M_HEADS, ln), lambda bi, i: (bi, 0, cidx(i)))]
    if final is not None:
        hb, xc, z, x, hn_g, skip, w_out = final
        d = x.shape[-1]
        args += [hb, xc, z, x, hn_g.reshape(1, c), skip.reshape(1, c), w_out]
        specs += [tile(c), tile(c), tile(c), tile(d), _const_spec((1, c)), _const_spec((1, c)),
                  _const_spec(w_out.shape)]
        out_w = d
    else:
        out_w = c
    return pl.pallas_call(
        functools.partial(_mlstm_body, rev, final is not None),
        grid=(b, nc),
        in_specs=specs,
        out_specs=tile(out_w),
        out_shape=jax.ShapeDtypeStruct((b, s, out_w), F32),
        scratch_shapes=[pltpu.VMEM((M_HEADS, dh, dh), F32), pltpu.VMEM((M_HEADS, 8, dh), F32),
                        pltpu.VMEM((M_HEADS, 8, 128), F32)],
        compiler_params=_params(2),
        name="mlstm_scan_bwd" if rev else "mlstm_scan_fwd",
    )(*args)


def _blockdiag_tiles(w):
    nb, bc, bd = w.shape
    per = PLE_GROUP // bc
    w = w.reshape(nb // per, per, bc, bd)
    eye = jnp.eye(per, dtype=w.dtype)
    dense = jnp.einsum('jncd,nm->jncmd', w, eye)
    return dense.reshape(nb // per, per * bc, per * bd)


def _mlstm_mixer(x, mix_g, w_in, conv_w, conv_b, wq, wk, wv, w_gate, b_gate, head_norm, skip, w_out):
    b, s, d = x.shape
    c = w_out.shape[0]
    xm, z = _mpre(x.reshape(b * s, d), mix_g, w_in.astype(BF16))
    xm, z = xm.reshape(b, s, c), z.reshape(b, s, c)
    wqk = jnp.concatenate([_blockdiag_tiles(wq), _blockdiag_tiles(wk)], axis=-1).astype(BF16)
    wvt = _blockdiag_tiles(wv).astype(BF16)
    ng = w_gate.shape[1]
    wg = jnp.pad(w_gate.reshape(3, c, ng), ((0, 0), (0, 0), (0, GATE_LANES - ng))).astype(BF16)
    bg = jnp.pad(b_gate.reshape(1, ng), ((0, 0), (0, GATE_LANES - ng)))
    k_scale = float(c // M_HEADS) ** -0.5
    q, k, v, xc, g, gt = _mconv(xm, conv_w, conv_b, wqk, wvt, wg, bg, k_scale)
    hb = _mlstm_scan(True, q, k, v, g, gt)
    return _mlstm_scan(False, q, k, v, g, gt, final=(hb, xc, z, x, head_norm, skip, w_out.astype(BF16)))


def _hpre_body(x_ref, g_ref, w_ref, bf_ref, lb_ref, q_ref, kf_ref, kb_ref, v_ref, lff_ref, lfb_ref, gg_ref):
    d = x_ref.shape[-1]
    nh = q_ref.shape[1]
    dk = d // nh
    xn = _rms(x_ref[0], g_ref[...]).astype(BF16)

    def heads(ref, val):
        for h in range(nh):
            ref[0, h] = val[:, h * dk:(h + 1) * dk].astype(ref.dtype)

    heads(q_ref, _dot(xn, w_ref[:, 0:d]))
    heads(v_ref, _dot(xn, w_ref[:, 3 * d:4 * d]))
    gg_ref[0] = _dot(xn, w_ref[:, 4 * d:5 * d]).astype(BF16)
    for di, (k_ref, lf_ref) in enumerate(((kf_ref, lff_ref), (kb_ref, lfb_ref))):
        a = _dot(xn, w_ref[:, (1 + di) * d:(2 + di) * d]) + bf_ref[di:di + 1, :]
        lb = lb_ref[di:di + 1, :]
        e = jnp.exp(-jnp.abs(a))
        inv = 1.0 / (1.0 + e)
        pos = a >= 0.0
        sig = jnp.where(pos, inv, e * inv)
        nsig = jnp.where(pos, e * inv, inv)
        heads(lf_ref, jnp.log(lb + (1.0 - lb) * sig))
        heads(k_ref, (1.0 - lb) * nsig)


def _hpre(x, g, w_in, b_f, lb):
    b, s, d = x.shape
    nh = H_HEADS
    dk = d // nh
    tm = TOKEN_TILE
    hm = pl.BlockSpec((1, nh, tm, dk), lambda bi, i: (bi, 0, i, 0))
    tile = pl.BlockSpec((1, tm, d), lambda bi, i: (bi, i, 0))
    hshape = lambda dt: jax.ShapeDtypeStruct((b, nh, s, dk), dt)
    return pl.pallas_call(
        _hpre_body,
        grid=(b, s // tm),
        in_specs=[tile, _const_spec((1, d)), _const_spec(w_in.shape), _const_spec((2, d)), _const_spec((2, d))],
        out_specs=[hm, hm, hm, hm, hm, hm, tile],
        out_shape=[hshape(BF16)] * 4 + [hshape(F32)] * 2 + [jax.ShapeDtypeStruct((b, s, d), BF16)],
        compiler_params=_params(2),
        name="hgrn_pre",
    )(x, g.reshape(1, d), w_in, b_f, lb)


def _hgrn_body(rev, q_ref, k_ref, v_ref, lf_ref, o_ref, st_ref):
    nh, ln, dk = q_ref.shape[1], q_ref.shape[2], q_ref.shape[3]

    @pl.when(pl.program_id(1) == 0)
    def _():
        st_ref[...] = jnp.zeros_like(st_ref)

    row = lax.broadcasted_iota(jnp.int32, (ln, dk), 0)
    tt = lax.broadcasted_iota(jnp.int32, (ln, ln), 0)
    ss = lax.broadcasted_iota(jnp.int32, (ln, ln), 1)
    xor_f = (tt ^ ss).astype(F32)
    lvl = (lax.bitcast_convert_type(xor_f, jnp.int32) >> 23) - 127
    lvl = jnp.where(tt == ss, -1, lvl)
    causal = (ss > tt) if rev else (ss < tt)
    lvl = jnp.where(causal | (tt == ss), lvl, -2)

    def head(h, carry):
        lf = lf_ref[0, h]
        q = q_ref[0, h]
        k = k_ref[0, h]
        v = v_ref[0, h]
        qf, kf = q.astype(F32), k.astype(F32)
        a = jnp.where(lvl == -1, _dot_nt(q, k), 0.0)
        pre, tot = lf, lf
        m, j = 1, 0
        while m < ln:
            upper = (row & m) != 0
            if rev:
                e = jnp.where(upper, pre - lf, tot - pre + lf)
            else:
                e = jnp.where(upper, pre, tot - pre)
            w = jnp.exp(e)
            a = jnp.where(lvl == j, _dot_nt((qf * w).astype(BF16), (kf * w).astype(BF16)), a)
            sib = jnp.where(upper, pltpu.roll(tot, m, 0), pltpu.roll(tot, ln - m, 0))
            pre = pre + jnp.where(upper, sib, 0.0)
            tot = tot + sib
            m, j = 2 * m, j + 1
        if rev:
            e_q, e_k = tot - pre + lf, pre - lf
        else:
            e_q, e_k = pre, tot - pre
        st = st_ref[h]
        o = _dot(a.astype(BF16), v) + _dot_nt((qf * jnp.exp(e_q)).astype(BF16), st.astype(BF16))
        o_ref[0, h] = o
        kw = (kf * jnp.exp(e_k)).astype(BF16)
        st_ref[h] = jnp.exp(tot[0:1, :]) * st + _dot_tn(v, kw)
        return carry

    lax.fori_loop(0, nh, head, 0)


def _hgrn_scan(rev, q, k, v, lf):
    b, nh, s, dk = q.shape
    ln = SCAN_CHUNK
    nc = s // ln
    cidx = (lambda i: nc - 1 - i) if rev else (lambda i: i)
    hm = pl.BlockSpec((1, nh, ln, dk), lambda bi, i: (bi, 0, cidx(i), 0))
    return pl.pallas_call(
        functools.partial(_hgrn_body, rev),
        grid=(b, nc),
        in_specs=[hm, hm, hm, hm],
        out_specs=hm,
        out_shape=jax.ShapeDtypeStruct((b, nh, s, dk), F32),
        scratch_shapes=[pltpu.VMEM((nh, dk, dk), F32)],
        compiler_params=_params(2),
        name="hgrn_scan_bwd" if rev else "hgrn_scan_fwd",
    )(q, k, v, lf)


def _hpost_body(of_ref, ob_ref, gg_ref, x_ref, hn_ref, wo_ref, o_ref):
    nh, dk = of_ref.shape[1], of_ref.shape[3]
    parts = []
    for h in range(nh):
        o = of_ref[0, h] + ob_ref[0, h]
        o = o * lax.rsqrt(jnp.mean(o * o, axis=-1, keepdims=True) + EPS) * hn_ref[:, h * dk:(h + 1) * dk]
        parts.append(o)
    o = jnp.concatenate(parts, axis=-1) * _sigmoid(gg_ref[0].astype(F32))
    o_ref[0] = x_ref[0] + _dot(o.astype(BF16), wo_ref[...])


def _hpost(o_f, o_b, gg, x, head_norm, w_out):
    b, nh, s, dk = o_f.shape
    d = x.shape[-1]
    tm = TOKEN_TILE
    hm = pl.BlockSpec((1, nh, tm, dk), lambda bi, i: (bi, 0, i, 0))
    tile = pl.BlockSpec((1, tm, d), lambda bi, i: (bi, i, 0))
    return pl.pallas_call(
        _hpost_body,
        grid=(b, s // tm),
        in_specs=[hm, hm, tile, tile, _const_spec((1, d)), _const_spec(w_out.shape)],
        out_specs=tile,
        out_shape=jax.ShapeDtypeStruct((b, s, d), F32),
        compiler_params=_params(2),
        name="hgrn_post",
    )(o_f, o_b, gg, x, head_norm.reshape(1, d), w_out)


def _lower_bounds(logits):
    pr = jax.nn.softmax(logits.astype(F32), axis=0)
    return jnp.cumsum(pr, axis=0) - pr[0]


def _hgrn_mixer(x, mix_g, w_in, b_f, lb, head_norm, w_out):
    q, k_f, k_b, v, lf_f, lf_b, gg = _hpre(x, mix_g, w_in.astype(BF16), b_f, lb)
    o_f = _hgrn_scan(False, q, k_f, v, lf_f)
    o_b = _hgrn_scan(True, q, k_b, v, lf_b)
    return _hpost(o_f, o_b, gg, x, head_norm, w_out.astype(BF16))


def _trunk(x, p, ffn1_norm, ffn1_w_in, ffn1_w_out, mix_norm,
           m_w_in, m_conv_w, m_conv_b, m_wq, m_wk, m_wv, m_w_gate, m_b_gate, m_head_norm, m_skip, m_w_out,
           h_w_in, h_b_f, h_lb_logits, h_head_norm, h_w_out,
           ffn2_norm, ffn2_w_in, ffn2_w_out, ple_norm, ple_w_gate, ple_w_proj, final_norm):
    b, s, d = x.shape
    depth = ffn1_norm.shape[0]
    n_mixers = 2
    lb_fwd = _lower_bounds(h_lb_logits[0])
    lb_bwd = _lower_bounds(h_lb_logits[1])
    flat = lambda t: t.reshape(b * s, t.shape[-1])
    for i in range(depth):
        x = _ffn(flat(x), ffn1_norm[i], ffn1_w_in[i].astype(BF16), ffn1_w_out[i].astype(BF16)).reshape(b, s, d)
        j = i // n_mixers
        if i % n_mixers == 0:
            x = _mlstm_mixer(x, mix_norm[i], m_w_in[j], m_conv_w[j], m_conv_b[j], m_wq[j], m_wk[j], m_wv[j],
                             m_w_gate[j], m_b_gate[j], m_head_norm[j], m_skip[j], m_w_out[j])
        else:
            lb = jnp.stack([lb_fwd[i], lb_bwd[i]])
            x = _hgrn_mixer(x, mix_norm[i], h_w_in[j], h_b_f[j], lb, h_head_norm[j], h_w_out[j])
        ple = (flat(p[i]), ple_norm[i], ple_w_gate[i].astype(BF16), ple_w_proj[i].astype(BF16))
        x = _ffn(flat(x), ffn2_norm[i], ffn2_w_in[i].astype(BF16), ffn2_w_out[i].astype(BF16), ple=ple,
                 final_g=final_norm if i == depth - 1 else None).reshape(b, s, d)
    return x


def kernel(x_prompt, x_sample, p_prompt, p_sample, ffn1_norm, ffn1_w_in, ffn1_w_out, mix_norm, m_w_in, m_conv_w, m_conv_b, m_wq, m_wk, m_wv, m_w_gate, m_b_gate, m_head_norm, m_skip, m_w_out, h_w_in, h_b_f, h_lb_logits, h_head_norm, h_w_out, ffn2_norm, ffn2_w_in, ffn2_w_out, ple_norm, ple_w_gate, ple_w_proj, final_norm):
    weights = (ffn1_norm, ffn1_w_in, ffn1_w_out, mix_norm,
               m_w_in, m_conv_w, m_conv_b, m_wq, m_wk, m_wv, m_w_gate, m_b_gate, m_head_norm, m_skip, m_w_out,
               h_w_in, h_b_f, h_lb_logits, h_head_norm, h_w_out,
               ffn2_norm, ffn2_w_in, ffn2_w_out, ple_norm, ple_w_gate, ple_w_proj, final_norm)
    return (_trunk(x_prompt, p_prompt, *weights), _trunk(x_sample, p_sample, *weights))
```

```python
import functools

import jax
import jax.numpy as jnp
from jax import lax
from jax.experimental import pallas as pl
from jax.experimental.pallas import tpu as pltpu

F32 = jnp.float32
BF16 = jnp.bfloat16
EPS = 1e-6
LOG2_E = 1.4426950408889634

M_HEADS = 4
M_QKV_BLOCK = 4
M_CONV = 5
H_HEADS = 8
PLE_GROUP = 256
TOKEN_TILE = 512
SCAN_CHUNK = 256
CONV_HALO = 16
GATE_LANES = 128
VMEM_LIMIT = 56 * 1024 * 1024


def _params(n_axes):
    return pltpu.CompilerParams(dimension_semantics=("arbitrary",) * n_axes,
                                vmem_limit_bytes=VMEM_LIMIT)


def _const_spec(shape):
    nd = len(shape)
    return pl.BlockSpec(shape, lambda *_: (0,) * nd, pipeline_mode=pl.Buffered(1))


def _rms(x, g):
    return x * lax.rsqrt(jnp.mean(x * x, axis=-1, keepdims=True) + EPS) * g


def _sigmoid(x):
    return 1.0 / (1.0 + jnp.exp(-x))


def _dot(a, b):
    return jnp.dot(a, b, preferred_element_type=F32)


def _dot_nt(a, b):
    return lax.dot_general(a, b, (((1,), (1,)), ((), ())), preferred_element_type=F32)


def _dot_tn(a, b):
    return lax.dot_general(a, b, (((0,), (0,)), ((), ())), preferred_element_type=F32)


def _ffn_body(has_ple, has_final, *refs):
    it = iter(refs)
    x_ref, g_ref, win_ref, wout_ref = (next(it) for _ in range(4))
    if has_ple:
        p_ref, pg_ref, pwg_ref, pwp_ref = (next(it) for _ in range(4))
    if has_final:
        fg_ref = next(it)
    o_ref = next(it)
    d_ff = wout_ref.shape[0]
    x = x_ref[...]
    xn = _rms(x, g_ref[...]).astype(BF16)
    a = _dot(xn, win_ref[:, :d_ff])
    u = _dot(xn, win_ref[:, d_ff:])
    act = (a * _sigmoid(a) * u).astype(BF16)
    x = x + 0.5 * _dot(act, wout_ref[...])
    if has_ple:
        xg = _rms(x, pg_ref[...]).astype(BF16)
        gate = _sigmoid(_dot(xg, pwg_ref[...]))
        x = x + gate * _dot(p_ref[...].astype(BF16), pwp_ref[...])
    if has_final:
        x = _rms(x, fg_ref[...])
    o_ref[...] = x


def _ffn(x, g, w_in, w_out, ple=None, final_g=None):
    t, d = x.shape
    tm = TOKEN_TILE
    row = lambda w: pl.BlockSpec((tm, w), lambda i: (i, 0))
    args = [x, g.reshape(1, d), w_in, w_out]
    specs = [row(d), _const_spec((1, d)), _const_spec(w_in.shape), _const_spec(w_out.shape)]
    if ple is not None:
        p, pg, pwg, pwp = ple
        args += [p, pg.reshape(1, d), pwg, pwp]
        specs += [row(p.shape[1]), _const_spec((1, d)), _const_spec(pwg.shape), _const_spec(pwp.shape)]
    if final_g is not None:
        args.append(final_g.reshape(1, d))
        specs.append(_const_spec((1, d)))
    return pl.pallas_call(
        functools.partial(_ffn_body, ple is not None, final_g is not None),
        grid=(t // tm,),
        in_specs=specs,
        out_specs=row(d),
        out_shape=jax.ShapeDtypeStruct((t, d), F32),
        compiler_params=_params(1),
        name="ffn",
    )(*args)


def _mpre_body(x_ref, g_ref, w_ref, xm_ref, z_ref):
    n = xm_ref.shape[-1]
    xn = _rms(x_ref[...], g_ref[...]).astype(BF16)
    xm_ref[...] = _dot(xn, w_ref[:, :n]).astype(BF16)
    z_ref[...] = _dot(xn, w_ref[:, n:]).astype(BF16)


def _mpre(x, g, w_in):
    t, d = x.shape
    n = w_in.shape[1] // 2
    tm = TOKEN_TILE
    row = lambda w: pl.BlockSpec((tm, w), lambda i: (i, 0))
    return pl.pallas_call(
        _mpre_body,
        grid=(t // tm,),
        in_specs=[row(d), _const_spec((1, d)), _const_spec(w_in.shape)],
        out_specs=[row(n), row(n)],
        out_shape=[jax.ShapeDtypeStruct((t, n), BF16)] * 2,
        compiler_params=_params(1),
        name="mlstm_pre",
    )(x, g.reshape(1, d), w_in)


def _log_sigmoid(x):
    return jnp.minimum(x, 0.0) - jnp.log1p(jnp.exp(-jnp.abs(x)))


def _mconv_body(k_scale, xm_ref, prev_ref, next_ref, cw_ref, cb_ref, wqk_ref, wv_ref, wg_ref, bg_ref,
                q_ref, k_ref, v_ref, xc_ref, g_ref, gt_ref, ext_ref):
    i = pl.program_id(1)
    last = pl.num_programs(1) - 1
    ts, c = xm_ref.shape[1], xm_ref.shape[2]
    hl = CONV_HALO
    pad = M_CONV // 2
    xm = xm_ref[0]
    ext_ref[0:hl, :] = jnp.where(i > 0, prev_ref[0].astype(F32), 0.0)
    ext_ref[hl:hl + ts, :] = xm.astype(F32)
    ext_ref[hl + ts:hl + ts + hl, :] = jnp.where(i < last, next_ref[0].astype(F32), 0.0)
    acc = jnp.broadcast_to(cb_ref[...], (ts, c))
    for j in range(M_CONV):
        acc = acc + cw_ref[j:j + 1, :] * ext_ref[hl - pad + j:hl - pad + j + ts, :]
    xc = (acc * _sigmoid(acc)).astype(BF16)
    xc_ref[0] = xc
    gw = PLE_GROUP
    gates = jnp.broadcast_to(bg_ref[...], (ts, GATE_LANES))
    for j in range(c // gw):
        cs = slice(j * gw, (j + 1) * gw)
        qk = _dot(xc[:, cs], wqk_ref[j])
        qj = qk[:, :gw].astype(BF16)
        kj = qk[:, gw:]
        kjb = kj.astype(BF16)
        vj = _dot(xm[:, cs], wv_ref[j]).astype(BF16)
        q_ref[0, :, cs] = qj
        k_ref[0, :, cs] = (kj * k_scale).astype(BF16)
        v_ref[0, :, cs] = vj
        gates = gates + _dot(qj, wg_ref[0, cs, :]) + _dot(kjb, wg_ref[1, cs, :]) + _dot(vj, wg_ref[2, cs, :])
    col = lax.broadcasted_iota(jnp.int32, gates.shape, 1)
    is_forget = (col % (2 * M_HEADS)) >= M_HEADS
    gates = jnp.where(is_forget, _log_sigmoid(gates), gates)
    g_ref[0] = gates
    gt_ref[0] = gates.T[:4 * M_HEADS, :]


def _mconv(xm, conv_w, conv_b, wqk, wv, wg, bg, k_scale):
    b, s, c = xm.shape
    ts = TOKEN_TILE
    hb = ts // CONV_HALO
    nh_blocks = s // CONV_HALO
    tile = lambda w, dt=None: pl.BlockSpec((1, ts, w), lambda bi, i: (bi, i, 0))
    return pl.pallas_call(
        functools.partial(_mconv_body, k_scale),
        grid=(b, s // ts),
        in_specs=[
            tile(c),
            pl.BlockSpec((1, CONV_HALO, c), lambda bi, i: (bi, jnp.maximum(i * hb - 1, 0), 0)),
            pl.BlockSpec((1, CONV_HALO, c), lambda bi, i: (bi, jnp.minimum((i + 1) * hb, nh_blocks - 1), 0)),
            _const_spec(conv_w.shape), _const_spec((1, c)),
            _const_spec(wqk.shape), _const_spec(wv.shape), _const_spec(wg.shape), _const_spec(bg.shape),
        ],
        out_specs=[tile(c), tile(c), tile(c), tile(c), tile(GATE_LANES),
                   pl.BlockSpec((1, 4 * M_HEADS, ts), lambda bi, i: (bi, 0, i))],
        out_shape=[jax.ShapeDtypeStruct((b, s, c), BF16)] * 4
        + [jax.ShapeDtypeStruct((b, s, GATE_LANES), F32), jax.ShapeDtypeStruct((b, 4 * M_HEADS, s), F32)],
        scratch_shapes=[pltpu.VMEM((ts + 2 * CONV_HALO, c), F32)],
        compiler_params=_params(2),
        name="mlstm_conv_qkv",
    )(xm, xm, xm, conv_w, conv_b.reshape(1, c), wqk, wv, wg, bg)


def _cumsum(x, axis, rev):
    n = x.shape[axis]
    idx = lax.broadcasted_iota(jnp.int32, x.shape, axis)
    s = 1
    while s < n:
        if rev:
            x = x + jnp.where(idx < n - s, pltpu.roll(x, n - s, axis), 0.0)
        else:
            x = x + jnp.where(idx >= s, pltpu.roll(x, s, axis), 0.0)
        s *= 2
    return x


def _mlstm_body(rev, final, *refs):
    if final:
        (q_ref, k_ref, v_ref, g_ref, gt_ref, hb_ref, xc_ref, z_ref, x_ref, hn_ref, sk_ref, wo_ref,
         o_ref, c_ref, n_ref, m_ref) = refs
    else:
        q_ref, k_ref, v_ref, g_ref, gt_ref, o_ref, c_ref, n_ref, m_ref = refs
    nh = M_HEADS
    ln = q_ref.shape[1]
    dh = q_ref.shape[2] // nh

    @pl.when(pl.program_id(1) == 0)
    def _():
        c_ref[...] = jnp.zeros_like(c_ref)
        n_ref[...] = jnp.zeros_like(n_ref)
        m_ref[...] = jnp.zeros_like(m_ref)

    g = g_ref[0]
    gt = gt_ref[0]
    bcol = _cumsum(g, 0, rev)
    brow = _cumsum(gt, 1, rev)
    tt = lax.broadcasted_iota(jnp.int32, (ln, ln), 0)
    ss = lax.broadcasted_iota(jnp.int32, (ln, ln), 1)
    valid = (ss >= tt) if rev else (ss <= tt)
    end = 0 if rev else ln - 1
    goff = 2 * nh if rev else 0
    if final:
        acc = jnp.zeros(o_ref.shape[1:], F32)
    for h in range(nh):
        hs = slice(h * dh, (h + 1) * dh)
        ci, cf = goff + h, goff + nh + h
        li_row, b_row = gt[ci:ci + 1, :], brow[cf:cf + 1, :]
        li_col, b_col = g[:, ci:ci + 1], bcol[:, cf:cf + 1]
        b_all = b_row[:, end:end + 1]
        m_prev = m_ref[h][0:1, 0:1]
        q, k, v = q_ref[0, :, hs], k_ref[0, :, hs], v_ref[0, :, hs]

        dlog = jnp.where(valid, b_col - b_row + li_row, -jnp.inf)
        m_t = jnp.maximum(b_col + m_prev, jnp.max(dlog, axis=-1, keepdims=True))
        s_mat = _dot_nt(q, k) * jnp.exp(dlog - m_t)
        w_inter = jnp.exp(b_col + m_prev - m_t)
        cst = c_ref[h]
        nst = n_ref[h][0:1, :]
        num = _dot(s_mat.astype(BF16), v) + w_inter * _dot(q, cst.astype(BF16))
        den = jnp.sum(s_mat, axis=-1, keepdims=True) + \
            w_inter * jnp.sum(q.astype(F32) * nst, axis=-1, keepdims=True)
        hh = num * (1.0 / jnp.maximum(jnp.abs(den), jnp.exp(-m_t)))

        a_row = b_all - b_row + li_row
        m_new = jnp.maximum(b_all + m_prev, jnp.max(a_row, axis=-1, keepdims=True))
        wk_col = jnp.exp(b_all - b_col + li_col - m_new)
        w_old = jnp.exp(b_all + m_prev - m_new)
        kw = k.astype(F32) * wk_col
        c_ref[h] = w_old * cst + _dot_tn(kw.astype(BF16), v)
        n_ref[h] = jnp.broadcast_to(w_old * nst + jnp.sum(kw, axis=0, keepdims=True), n_ref.shape[1:])
        m_ref[h] = jnp.broadcast_to(m_new, m_ref.shape[1:])

        if final:
            hsum = hh + hb_ref[0, :, hs]
            hn = hsum * lax.rsqrt(jnp.mean(hsum * hsum, axis=-1, keepdims=True) + EPS) * hn_ref[:, hs]
            y = hn + sk_ref[:, hs] * xc_ref[0, :, hs].astype(F32)
            z = z_ref[0, :, hs].astype(F32)
            y = y * (z * _sigmoid(z))
            acc = acc + _dot(y.astype(BF16), wo_ref[hs, :])
        else:
            o_ref[0, :, hs] = hh
    if final:
        o_ref[0] = x_ref[0] + acc


def _mlstm_scan(rev, q, k, v, g, gt, final=None):
    b, s, c = q.shape
    ln = SCAN_CHUNK
    nc = s // ln
    dh = c // M_HEADS
    cidx = (lambda i: nc - 1 - i) if rev else (lambda i: i)
    tile = lambda w: pl.BlockSpec((1, ln, w), lambda bi, i: (bi, cidx(i), 0))
    args = [q, k, v, g, gt]
    specs = [tile(c), tile(c), tile(c), tile(GATE_LANES),
             pl.BlockSpec((1, 4 * M_HEADS, ln), lambda bi, i: (bi, 0, cidx(i)))]
    if final is not None:
        hb, xc, z, x, hn_g, skip, w_out = final
        d = x.shape[-1]
        args += [hb, xc, z, x, hn_g.reshape(1, c), skip.reshape(1, c), w_out]
        specs += [tile(c), tile(c), tile(c), tile(d), _const_spec((1, c)), _const_spec((1, c)),
                  _const_spec(w_out.shape)]
        out_w = d
    else:
        out_w = c
    return pl.pallas_call(
        functools.partial(_mlstm_body, rev, final is not None),
        grid=(b, nc),
        in_specs=specs,
        out_specs=tile(out_w),
        out_shape=jax.ShapeDtypeStruct((b, s, out_w), F32),
        scratch_shapes=[pltpu.VMEM((M_HEADS, dh, dh), F32), pltpu.VMEM((M_HEADS, 8, dh), F32),
                        pltpu.VMEM((M_HEADS, 8, 128), F32)],
        compiler_params=_params(2),
        name="mlstm_scan_bwd" if rev else "mlstm_scan_fwd",
    )(*args)


def _blockdiag_tiles(w):
    nb, bc, bd = w.shape
    per = PLE_GROUP // bc
    w = w.reshape(nb // per, per, bc, bd)
    eye = jnp.eye(per, dtype=w.dtype)
    dense = jnp.einsum('jncd,nm->jncmd', w, eye)
    return dense.reshape(nb // per, per * bc, per * bd)


def _mlstm_mixer(x, mix_g, w_in, conv_w, conv_b, wq, wk, wv, w_gate, b_gate, head_norm, skip, w_out):
    b, s, d = x.shape
    c = w_out.shape[0]
    xm, z = _mpre(x.reshape(b * s, d), mix_g, w_in.astype(BF16))
    xm, z = xm.reshape(b, s, c), z.reshape(b, s, c)
    wqk = jnp.concatenate([_blockdiag_tiles(wq), _blockdiag_tiles(wk)], axis=-1).astype(BF16)
    wvt = _blockdiag_tiles(wv).astype(BF16)
    ng = w_gate.shape[1]
    wg = jnp.pad(w_gate.reshape(3, c, ng), ((0, 0), (0, 0), (0, GATE_LANES - ng))).astype(BF16)
    bg = jnp.pad(b_gate.reshape(1, ng), ((0, 0), (0, GATE_LANES - ng)))
    k_scale = float(c // M_HEADS) ** -0.5
    q, k, v, xc, g, gt = _mconv(xm, conv_w, conv_b, wqk, wvt, wg, bg, k_scale)
    hb = _mlstm_scan(True, q, k, v, g, gt)
    return _mlstm_scan(False, q, k, v, g, gt, final=(hb, xc, z, x, head_norm, skip, w_out.astype(BF16)))


def _hpre_body(x_ref, g_ref, w_ref, bf_ref, lb_ref, q_ref, kf_ref, kb_ref, v_ref, lff_ref, lfb_ref, gg_ref):
    d = x_ref.shape[-1]
    nh = q_ref.shape[1]
    dk = d // nh
    xn = _rms(x_ref[0], g_ref[...]).astype(BF16)

    def heads(ref, val):
        for h in range(nh):
            ref[0, h] = val[:, h * dk:(h + 1) * dk].astype(ref.dtype)

    heads(q_ref, _dot(xn, w_ref[:, 0:d]))
    heads(v_ref, _dot(xn, w_ref[:, 3 * d:4 * d]))
    gg_ref[0] = _dot(xn, w_ref[:, 4 * d:5 * d]).astype(BF16)
    for di, (k_ref, lf_ref) in enumerate(((kf_ref, lff_ref), (kb_ref, lfb_ref))):
        a = _dot(xn, w_ref[:, (1 + di) * d:(2 + di) * d]) + bf_ref[di:di + 1, :]
        lb = lb_ref[di:di + 1, :]
        e = jnp.exp(-jnp.abs(a))
        inv = 1.0 / (1.0 + e)
        pos = a >= 0.0
        sig = jnp.where(pos, inv, e * inv)
        nsig = jnp.where(pos, e * inv, inv)
        heads(lf_ref, jnp.log(lb + (1.0 - lb) * sig) * LOG2_E)
        heads(k_ref, (1.0 - lb) * nsig)


def _hpre(x, g, w_in, b_f, lb):
    b, s, d = x.shape
    nh = H_HEADS
    dk = d // nh
    tm = TOKEN_TILE
    hm = pl.BlockSpec((1, nh, tm, dk), lambda bi, i: (bi, 0, i, 0))
    tile = pl.BlockSpec((1, tm, d), lambda bi, i: (bi, i, 0))
    hshape = lambda dt: jax.ShapeDtypeStruct((b, nh, s, dk), dt)
    return pl.pallas_call(
        _hpre_body,
        grid=(b, s // tm),
        in_specs=[tile, _const_spec((1, d)), _const_spec(w_in.shape), _const_spec((2, d)), _const_spec((2, d))],
        out_specs=[hm, hm, hm, hm, hm, hm, tile],
        out_shape=[hshape(BF16)] * 4 + [hshape(F32)] * 2 + [jax.ShapeDtypeStruct((b, s, d), BF16)],
        compiler_params=_params(2),
        name="hgrn_pre",
    )(x, g.reshape(1, d), w_in, b_f, lb)


def _hgrn_body(rev, q_ref, k_ref, v_ref, lf_ref, o_ref, st_ref):
    nh, ln, dk = q_ref.shape[1], q_ref.shape[2], q_ref.shape[3]
    hb = ln // 2
    sub = 8

    @pl.when(pl.program_id(1) == 0)
    def _():
        st_ref[...] = jnp.zeros_like(st_ref)

    row = lax.broadcasted_iota(jnp.int32, (ln, dk), 0)
    tt = lax.broadcasted_iota(jnp.int32, (hb, hb), 0)
    ss = lax.broadcasted_iota(jnp.int32, (hb, hb), 1)
    lvl = (lax.bitcast_convert_type((tt ^ ss).astype(F32), jnp.int32) >> 23) - 127
    lvl = jnp.where((ss > tt) if rev else (ss < tt), lvl, jnp.where(tt == ss, -1, -2))
    halves = (slice(0, hb), slice(hb, ln))

    def head(h):
        lf = lf_ref[0, h]
        q, k, v = q_ref[0, h], k_ref[0, h], v_ref[0, h]
        qf, kf = q.astype(F32), k.astype(F32)
        zeros = jnp.zeros_like(lf)
        z_up, z_lo = (kf, qf) if rev else (qf, kf)
        x, y = (zeros, lf) if rev else (lf, zeros)
        tot = lf
        a = [jnp.where(lvl == -1, _dot_nt(q[hv], k[hv]), 0.0) for hv in halves]
        m, j = 1, 0
        while m < ln:
            if m < sub:
                upper = (row & m) != 0
                w = jnp.where(upper, z_up, z_lo) * jnp.exp2(jnp.where(upper, x, y))
                sib = jnp.where(upper, pltpu.roll(tot, m, 0), pltpu.roll(tot, ln - m, 0))
                x = x + jnp.where(upper, sib, 0.0)
                y = y + jnp.where(upper, 0.0, sib)
                tot = tot + sib
            else:
                nb = ln // (2 * m)
                split = lambda t: t.reshape(nb, 2, m, dk)
                join = lambda lo, up: jnp.stack([lo, up], axis=1).reshape(ln, dk)
                x4, y4, t4 = split(x), split(y), split(tot)
                w = join(split(z_lo)[:, 0] * jnp.exp2(y4[:, 0]), split(z_up)[:, 1] * jnp.exp2(x4[:, 1]))
                x = join(x4[:, 0], x4[:, 1] + t4[:, 0])
                y = join(y4[:, 0] + t4[:, 1], y4[:, 1])
                tsum = t4[:, 0] + t4[:, 1]
                tot = join(tsum, tsum)
            w = w.astype(BF16)
            if m < hb:
                for i, hv in enumerate(halves):
                    a[i] = jnp.where(lvl == j, _dot_nt(w[hv], w[hv]), a[i])
            else:
                cross = _dot_nt(w[halves[0]], w[halves[1]]) if rev else _dot_nt(w[halves[1]], w[halves[0]])
            m, j = 2 * m, j + 1
        e_q, e_k = (y, x) if rev else (x, y)
        st = st_ref[h]
        inter = _dot_nt((qf * jnp.exp2(e_q)).astype(BF16), st.astype(BF16))
        a0, a1, cr = a[0].astype(BF16), a[1].astype(BF16), cross.astype(BF16)
        if rev:
            o0 = _dot(jnp.concatenate([a0, cr], axis=1), v)
            o1 = _dot(a1, v[halves[1]])
        else:
            o0 = _dot(a0, v[halves[0]])
            o1 = _dot(jnp.concatenate([cr, a1], axis=1), v)
        o_ref[0, h, halves[0], :] = o0 + inter[halves[0]]
        o_ref[0, h, halves[1], :] = o1 + inter[halves[1]]
        kw = (kf * jnp.exp2(e_k)).astype(BF16)
        st_ref[h] = jnp.exp2(tot[0:1, :]) * st + _dot_tn(v, kw)

    def pair(i, carry):
        head(2 * i)
        head(2 * i + 1)
        return carry

    lax.fori_loop(0, nh // 2, pair, 0)


def _hgrn_scan(rev, q, k, v, lf):
    b, nh, s, dk = q.shape
    ln = SCAN_CHUNK
    nc = s // ln
    cidx = (lambda i: nc - 1 - i) if rev else (lambda i: i)
    hm = pl.BlockSpec((1, nh, ln, dk), lambda bi, i: (bi, 0, cidx(i), 0))
    return pl.pallas_call(
        functools.partial(_hgrn_body, rev),
        grid=(b, nc),
        in_specs=[hm, hm, hm, hm],
        out_specs=hm,
        out_shape=jax.ShapeDtypeStruct((b, nh, s, dk), F32),
        scratch_shapes=[pltpu.VMEM((nh, dk, dk), F32)],
        compiler_params=_params(2),
        name="hgrn_scan_bwd" if rev else "hgrn_scan_fwd",
    )(q, k, v, lf)


def _hpost_body(of_ref, ob_ref, gg_ref, x_ref, hn_ref, wo_ref, o_ref):
    nh, dk = of_ref.shape[1], of_ref.shape[3]
    parts = []
    for h in range(nh):
        o = of_ref[0, h] + ob_ref[0, h]
        o = o * lax.rsqrt(jnp.mean(o * o, axis=-1, keepdims=True) + EPS) * hn_ref[:, h * dk:(h + 1) * dk]
        parts.append(o)
    o = jnp.concatenate(parts, axis=-1) * _sigmoid(gg_ref[0].astype(F32))
    o_ref[0] = x_ref[0] + _dot(o.astype(BF16), wo_ref[...])


def _hpost(o_f, o_b, gg, x, head_norm, w_out):
    b, nh, s, dk = o_f.shape
    d = x.shape[-1]
    tm = TOKEN_TILE
    hm = pl.BlockSpec((1, nh, tm, dk), lambda bi, i: (bi, 0, i, 0))
    tile = pl.BlockSpec((1, tm, d), lambda bi, i: (bi, i, 0))
    return pl.pallas_call(
        _hpost_body,
        grid=(b, s // tm),
        in_specs=[hm, hm, tile, tile, _const_spec((1, d)), _const_spec(w_out.shape)],
        out_specs=tile,
        out_shape=jax.ShapeDtypeStruct((b, s, d), F32),
        compiler_params=_params(2),
        name="hgrn_post",
    )(o_f, o_b, gg, x, head_norm.reshape(1, d), w_out)


def _lower_bounds(logits):
    pr = jax.nn.softmax(logits.astype(F32), axis=0)
    return jnp.cumsum(pr, axis=0) - pr[0]


def _hgrn_mixer(x, mix_g, w_in, b_f, lb, head_norm, w_out):
    q, k_f, k_b, v, lf_f, lf_b, gg = _hpre(x, mix_g, w_in.astype(BF16), b_f, lb)
    o_f = _hgrn_scan(False, q, k_f, v, lf_f)
    o_b = _hgrn_scan(True, q, k_b, v, lf_b)
    return _hpost(o_f, o_b, gg, x, head_norm, w_out.astype(BF16))


def _trunk(x, p, ffn1_norm, ffn1_w_in, ffn1_w_out, mix_norm,
           m_w_in, m_conv_w, m_conv_b, m_wq, m_wk, m_wv, m_w_gate, m_b_gate, m_head_norm, m_skip, m_w_out,
           h_w_in, h_b_f, h_lb_logits, h_head_norm, h_w_out,
           ffn2_norm, ffn2_w_in, ffn2_w_out, ple_norm, ple_w_gate, ple_w_proj, final_norm):
    b, s, d = x.shape
    depth = ffn1_norm.shape[0]
    n_mixers = 2
    lb_fwd = _lower_bounds(h_lb_logits[0])
    lb_bwd = _lower_bounds(h_lb_logits[1])
    flat = lambda t: t.reshape(b * s, t.shape[-1])
    for i in range(depth):
        x = _ffn(flat(x), ffn1_norm[i], ffn1_w_in[i].astype(BF16), ffn1_w_out[i].astype(BF16)).reshape(b, s, d)
        j = i // n_mixers
        if i % n_mixers == 0:
            x = _mlstm_mixer(x, mix_norm[i], m_w_in[j], m_conv_w[j], m_conv_b[j], m_wq[j], m_wk[j], m_wv[j],
                             m_w_gate[j], m_b_gate[j], m_head_norm[j], m_skip[j], m_w_out[j])
        else:
            lb = jnp.stack([lb_fwd[i], lb_bwd[i]])
            x = _hgrn_mixer(x, mix_norm[i], h_w_in[j], h_b_f[j], lb, h_head_norm[j], h_w_out[j])
        ple = (flat(p[i]), ple_norm[i], ple_w_gate[i].astype(BF16), ple_w_proj[i].astype(BF16))
        x = _ffn(flat(x), ffn2_norm[i], ffn2_w_in[i].astype(BF16), ffn2_w_out[i].astype(BF16), ple=ple,
                 final_g=final_norm if i == depth - 1 else None).reshape(b, s, d)
    return x


def kernel(x_prompt, x_sample, p_prompt, p_sample, ffn1_norm, ffn1_w_in, ffn1_w_out, mix_norm, m_w_in, m_conv_w, m_conv_b, m_wq, m_wk, m_wv, m_w_gate, m_b_gate, m_head_norm, m_skip, m_w_out, h_w_in, h_b_f, h_lb_logits, h_head_norm, h_w_out, ffn2_norm, ffn2_w_in, ffn2_w_out, ple_norm, ple_w_gate, ple_w_proj, final_norm):
    weights = (ffn1_norm, ffn1_w_in, ffn1_w_out, mix_norm,
               m_w_in, m_conv_w, m_conv_b, m_wq, m_wk, m_wv, m_w_gate, m_b_gate, m_head_norm, m_skip, m_w_out,
               h_w_in, h_b_f, h_lb_logits, h_head_norm, h_w_out,
               ffn2_norm, ffn2_w_in, ffn2_w_out, ple_norm, ple_w_gate, ple_w_proj, final_norm)
    return (_trunk(x_prompt, p_prompt, *weights), _trunk(x_sample, p_sample, *weights))
```

```python
import functools

import jax
import jax.numpy as jnp
from jax import lax
from jax.experimental import pallas as pl
from jax.experimental.pallas import tpu as pltpu

F32 = jnp.float32
BF16 = jnp.bfloat16
EPS = 1e-6
LOG2_E = 1.4426950408889634

M_HEADS = 4
M_QKV_BLOCK = 4
M_CONV = 5
H_HEADS = 8
H_UNROLL = 8
PLE_GROUP = 256
TOKEN_TILE = 512
SCAN_CHUNK = 256
CONV_HALO = 16
GATE_LANES = 128
VMEM_LIMIT = 56 * 1024 * 1024


def _params(n_axes):
    return pltpu.CompilerParams(dimension_semantics=("arbitrary",) * n_axes,
                                vmem_limit_bytes=VMEM_LIMIT)


def _const_spec(shape):
    nd = len(shape)
    return pl.BlockSpec(shape, lambda *_: (0,) * nd, pipeline_mode=pl.Buffered(1))


def _rms(x, g):
    return x * lax.rsqrt(jnp.mean(x * x, axis=-1, keepdims=True) + EPS) * g


def _sigmoid(x):
    return 1.0 / (1.0 + jnp.exp(-x))


def _dot(a, b):
    return jnp.dot(a, b, preferred_element_type=F32)


def _dot_nt(a, b):
    return lax.dot_general(a, b, (((1,), (1,)), ((), ())), preferred_element_type=F32)


def _dot_tn(a, b):
    return lax.dot_general(a, b, (((0,), (0,)), ((), ())), preferred_element_type=F32)


def _ffn_body(has_ple, has_final, *refs):
    it = iter(refs)
    x_ref, g_ref, win_ref, wout_ref = (next(it) for _ in range(4))
    if has_ple:
        p_ref, pg_ref, pwg_ref, pwp_ref = (next(it) for _ in range(4))
    if has_final:
        fg_ref = next(it)
    o_ref = next(it)
    d_ff = wout_ref.shape[0]
    x = x_ref[...]
    xn = _rms(x, g_ref[...]).astype(BF16)
    a = _dot(xn, win_ref[:, :d_ff])
    u = _dot(xn, win_ref[:, d_ff:])
    act = (a * _sigmoid(a) * u).astype(BF16)
    x = x + 0.5 * _dot(act, wout_ref[...])
    if has_ple:
        xg = _rms(x, pg_ref[...]).astype(BF16)
        gate = _sigmoid(_dot(xg, pwg_ref[...]))
        x = x + gate * _dot(p_ref[...].astype(BF16), pwp_ref[...])
    if has_final:
        x = _rms(x, fg_ref[...])
    o_ref[...] = x


def _ffn(x, g, w_in, w_out, ple=None, final_g=None):
    t, d = x.shape
    tm = TOKEN_TILE
    row = lambda w: pl.BlockSpec((tm, w), lambda i: (i, 0))
    args = [x, g.reshape(1, d), w_in, w_out]
    specs = [row(d), _const_spec((1, d)), _const_spec(w_in.shape), _const_spec(w_out.shape)]
    if ple is not None:
        p, pg, pwg, pwp = ple
        args += [p, pg.reshape(1, d), pwg, pwp]
        specs += [row(p.shape[1]), _const_spec((1, d)), _const_spec(pwg.shape), _const_spec(pwp.shape)]
    if final_g is not None:
        args.append(final_g.reshape(1, d))
        specs.append(_const_spec((1, d)))
    return pl.pallas_call(
        functools.partial(_ffn_body, ple is not None, final_g is not None),
        grid=(t // tm,),
        in_specs=specs,
        out_specs=row(d),
        out_shape=jax.ShapeDtypeStruct((t, d), F32),
        compiler_params=_params(1),
        name="ffn",
    )(*args)


def _mpre_body(x_ref, g_ref, w_ref, xm_ref, z_ref):
    n = xm_ref.shape[-1]
    xn = _rms(x_ref[...], g_ref[...]).astype(BF16)
    xm_ref[...] = _dot(xn, w_ref[:, :n]).astype(BF16)
    z_ref[...] = _dot(xn, w_ref[:, n:]).astype(BF16)


def _mpre(x, g, w_in):
    t, d = x.shape
    n = w_in.shape[1] // 2
    tm = TOKEN_TILE
    row = lambda w: pl.BlockSpec((tm, w), lambda i: (i, 0))
    return pl.pallas_call(
        _mpre_body,
        grid=(t // tm,),
        in_specs=[row(d), _const_spec((1, d)), _const_spec(w_in.shape)],
        out_specs=[row(n), row(n)],
        out_shape=[jax.ShapeDtypeStruct((t, n), BF16)] * 2,
        compiler_params=_params(1),
        name="mlstm_pre",
    )(x, g.reshape(1, d), w_in)


def _log_sigmoid(x):
    return jnp.minimum(x, 0.0) - jnp.log1p(jnp.exp(-jnp.abs(x)))


def _mconv_body(k_scale, xm_ref, prev_ref, next_ref, cw_ref, cb_ref, wqk_ref, wv_ref, wg_ref, bg_ref,
                q_ref, k_ref, v_ref, xc_ref, g_ref, gt_ref, ext_ref):
    i = pl.program_id(1)
    last = pl.num_programs(1) - 1
    ts, c = xm_ref.shape[1], xm_ref.shape[2]
    hl = CONV_HALO
    pad = M_CONV // 2
    xm = xm_ref[0]
    ext_ref[0:hl, :] = jnp.where(i > 0, prev_ref[0].astype(F32), 0.0)
    ext_ref[hl:hl + ts, :] = xm.astype(F32)
    ext_ref[hl + ts:hl + ts + hl, :] = jnp.where(i < last, next_ref[0].astype(F32), 0.0)
    acc = jnp.broadcast_to(cb_ref[...], (ts, c))
    for j in range(M_CONV):
        acc = acc + cw_ref[j:j + 1, :] * ext_ref[hl - pad + j:hl - pad + j + ts, :]
    xc = (acc * _sigmoid(acc)).astype(BF16)
    xc_ref[0] = xc
    gw = PLE_GROUP
    gates = jnp.broadcast_to(bg_ref[...], (ts, GATE_LANES))
    for j in range(c // gw):
        cs = slice(j * gw, (j + 1) * gw)
        qk = _dot(xc[:, cs], wqk_ref[j])
        qj = qk[:, :gw].astype(BF16)
        kj = qk[:, gw:]
        kjb = kj.astype(BF16)
        vj = _dot(xm[:, cs], wv_ref[j]).astype(BF16)
        q_ref[0, :, cs] = qj
        k_ref[0, :, cs] = (kj * k_scale).astype(BF16)
        v_ref[0, :, cs] = vj
        gates = gates + _dot(qj, wg_ref[0, cs, :]) + _dot(kjb, wg_ref[1, cs, :]) + _dot(vj, wg_ref[2, cs, :])
    col = lax.broadcasted_iota(jnp.int32, gates.shape, 1)
    is_forget = (col % (2 * M_HEADS)) >= M_HEADS
    gates = jnp.where(is_forget, _log_sigmoid(gates), gates)
    g_ref[0] = gates
    gt_ref[0] = gates.T[:4 * M_HEADS, :]


def _mconv(xm, conv_w, conv_b, wqk, wv, wg, bg, k_scale):
    b, s, c = xm.shape
    ts = TOKEN_TILE
    hb = ts // CONV_HALO
    nh_blocks = s // CONV_HALO
    tile = lambda w, dt=None: pl.BlockSpec((1, ts, w), lambda bi, i: (bi, i, 0))
    return pl.pallas_call(
        functools.partial(_mconv_body, k_scale),
        grid=(b, s // ts),
        in_specs=[
            tile(c),
            pl.BlockSpec((1, CONV_HALO, c), lambda bi, i: (bi, jnp.maximum(i * hb - 1, 0), 0)),
            pl.BlockSpec((1, CONV_HALO, c), lambda bi, i: (bi, jnp.minimum((i + 1) * hb, nh_blocks - 1), 0)),
            _const_spec(conv_w.shape), _const_spec((1, c)),
            _const_spec(wqk.shape), _const_spec(wv.shape), _const_spec(wg.shape), _const_spec(bg.shape),
        ],
        out_specs=[tile(c), tile(c), tile(c), tile(c), tile(GATE_LANES),
                   pl.BlockSpec((1, 4 * M_HEADS, ts), lambda bi, i: (bi, 0, i))],
        out_shape=[jax.ShapeDtypeStruct((b, s, c), BF16)] * 4
        + [jax.ShapeDtypeStruct((b, s, GATE_LANES), F32), jax.ShapeDtypeStruct((b, 4 * M_HEADS, s), F32)],
        scratch_shapes=[pltpu.VMEM((ts + 2 * CONV_HALO, c), F32)],
        compiler_params=_params(2),
        name="mlstm_conv_qkv",
    )(xm, xm, xm, conv_w, conv_b.reshape(1, c), wqk, wv, wg, bg)


def _cumsum(x, axis, rev):
    n = x.shape[axis]
    idx = lax.broadcasted_iota(jnp.int32, x.shape, axis)
    s = 1
    while s < n:
        if rev:
            x = x + jnp.where(idx < n - s, pltpu.roll(x, n - s, axis), 0.0)
        else:
            x = x + jnp.where(idx >= s, pltpu.roll(x, s, axis), 0.0)
        s *= 2
    return x


def _mlstm_body(rev, final, *refs):
    if final:
        (q_ref, k_ref, v_ref, g_ref, gt_ref, hb_ref, xc_ref, z_ref, x_ref, hn_ref, sk_ref, wo_ref,
         o_ref, c_ref, n_ref, m_ref) = refs
    else:
        q_ref, k_ref, v_ref, g_ref, gt_ref, o_ref, c_ref, n_ref, m_ref = refs
    nh = M_HEADS
    ln = q_ref.shape[1]
    dh = q_ref.shape[2] // nh

    def zero_state(h, carry):
        c_ref[h] = jnp.zeros(c_ref.shape[1:], F32)
        n_ref[h] = jnp.zeros(n_ref.shape[1:], F32)
        m_ref[h] = jnp.zeros(m_ref.shape[1:], F32)
        return carry

    lax.fori_loop(0, jnp.where(pl.program_id(1) == 0, nh, 0), zero_state, 0)

    g = g_ref[0]
    gt = gt_ref[0]
    bcol = _cumsum(g, 0, rev)
    brow = _cumsum(gt, 1, rev)
    tt = lax.broadcasted_iota(jnp.int32, (ln, ln), 0)
    ss = lax.broadcasted_iota(jnp.int32, (ln, ln), 1)
    valid = (ss >= tt) if rev else (ss <= tt)
    end = 0 if rev else ln - 1
    goff = 2 * nh if rev else 0
    if final:
        acc = jnp.zeros(o_ref.shape[1:], F32)
    for h in range(nh):
        hs = slice(h * dh, (h + 1) * dh)
        ci, cf = goff + h, goff + nh + h
        li_row, b_row = gt[ci:ci + 1, :], brow[cf:cf + 1, :]
        li_col, b_col = g[:, ci:ci + 1], bcol[:, cf:cf + 1]
        b_all = b_row[:, end:end + 1]
        m_prev = m_ref[h][0:1, 0:1]
        q, k, v = q_ref[0, :, hs], k_ref[0, :, hs], v_ref[0, :, hs]

        dlog = jnp.where(valid, b_col - b_row + li_row, -jnp.inf)
        m_t = jnp.maximum(b_col + m_prev, jnp.max(dlog, axis=-1, keepdims=True))
        s_mat = _dot_nt(q, k) * jnp.exp(dlog - m_t)
        w_inter = jnp.exp(b_col + m_prev - m_t)
        cst = c_ref[h]
        nst = n_ref[h][0:1, :]
        num = _dot(s_mat.astype(BF16), v) + w_inter * _dot(q, cst.astype(BF16))
        den = jnp.sum(s_mat, axis=-1, keepdims=True) + \
            w_inter * jnp.sum(q.astype(F32) * nst, axis=-1, keepdims=True)
        hh = num * (1.0 / jnp.maximum(jnp.abs(den), jnp.exp(-m_t)))

        a_row = b_all - b_row + li_row
        m_new = jnp.maximum(b_all + m_prev, jnp.max(a_row, axis=-1, keepdims=True))
        wk_col = jnp.exp(b_all - b_col + li_col - m_new)
        w_old = jnp.exp(b_all + m_prev - m_new)
        kw = k.astype(F32) * wk_col
        c_ref[h] = w_old * cst + _dot_tn(kw.astype(BF16), v)
        n_ref[h] = jnp.broadcast_to(w_old * nst + jnp.sum(kw, axis=0, keepdims=True), n_ref.shape[1:])
        m_ref[h] = jnp.broadcast_to(m_new, m_ref.shape[1:])

        if final:
            hsum = hh + hb_ref[0, :, hs]
            hn = hsum * lax.rsqrt(jnp.mean(hsum * hsum, axis=-1, keepdims=True) + EPS) * hn_ref[:, hs]
            y = hn + sk_ref[:, hs] * xc_ref[0, :, hs].astype(F32)
            z = z_ref[0, :, hs].astype(F32)
            y = y * (z * _sigmoid(z))
            acc = acc + _dot(y.astype(BF16), wo_ref[hs, :])
        else:
            o_ref[0, :, hs] = hh
    if final:
        o_ref[0] = x_ref[0] + acc


def _mlstm_scan(rev, q, k, v, g, gt, final=None):
    b, s, c = q.shape
    ln = SCAN_CHUNK
    nc = s // ln
    dh = c // M_HEADS
    cidx = (lambda i: nc - 1 - i) if rev else (lambda i: i)
    tile = lambda w: pl.BlockSpec((1, ln, w), lambda bi, i: (bi, cidx(i), 0))
    args = [q, k, v, g, gt]
    specs = [tile(c), tile(c), tile(c), tile(GATE_LANES),
             pl.BlockSpec((1, 4 * M_HEADS, ln), lambda bi, i: (bi, 0, cidx(i)))]
    if final is not None:
        hb, xc, z, x, hn_g, skip, w_out = final
        d = x.shape[-1]
        args += [hb, xc, z, x, hn_g.reshape(1, c), skip.reshape(1, c), w_out]
        specs += [tile(c), tile(c), tile(c), tile(d), _const_spec((1, c)), _const_spec((1, c)),
                  _const_spec(w_out.shape)]
        out_w = d
    else:
        out_w = c
    return pl.pallas_call(
        functools.partial(_mlstm_body, rev, final is not None),
        grid=(b, nc),
        in_specs=specs,
        out_specs=tile(out_w),
        out_shape=jax.ShapeDtypeStruct((b, s, out_w), F32),
        scratch_shapes=[pltpu.VMEM((M_HEADS, dh, dh), F32), pltpu.VMEM((M_HEADS, 8, dh), F32),
                        pltpu.VMEM((M_HEADS, 8, 128), F32)],
        compiler_params=_params(2),
        name="mlstm_scan_bwd" if rev else "mlstm_scan_fwd",
    )(*args)


def _blockdiag_tiles(w):
    nb, bc, bd = w.shape
    per = PLE_GROUP // bc
    w = w.reshape(nb // per, per, bc, bd)
    eye = jnp.eye(per, dtype=w.dtype)
    dense = jnp.einsum('jncd,nm->jncmd', w, eye)
    return dense.reshape(nb // per, per * bc, per * bd)


def _mlstm_mixer(x, mix_g, w_in, conv_w, conv_b, wq, wk, wv, w_gate, b_gate, head_norm, skip, w_out):
    b, s, d = x.shape
    c = w_out.shape[0]
    xm, z = _mpre(x.reshape(b * s, d), mix_g, w_in.astype(BF16))
    xm, z = xm.reshape(b, s, c), z.reshape(b, s, c)
    wqk = jnp.concatenate([_blockdiag_tiles(wq), _blockdiag_tiles(wk)], axis=-1).astype(BF16)
    wvt = _blockdiag_tiles(wv).astype(BF16)
    ng = w_gate.shape[1]
    wg = jnp.pad(w_gate.reshape(3, c, ng), ((0, 0), (0, 0), (0, GATE_LANES - ng))).astype(BF16)
    bg = jnp.pad(b_gate.reshape(1, ng), ((0, 0), (0, GATE_LANES - ng)))
    k_scale = float(c // M_HEADS) ** -0.5
    q, k, v, xc, g, gt = _mconv(xm, conv_w, conv_b, wqk, wvt, wg, bg, k_scale)
    hb = _mlstm_scan(True, q, k, v, g, gt)
    return _mlstm_scan(False, q, k, v, g, gt, final=(hb, xc, z, x, head_norm, skip, w_out.astype(BF16)))


def _hpre_body(x_ref, g_ref, w_ref, bf_ref, lb_ref, q_ref, kf_ref, kb_ref, v_ref, lff_ref, lfb_ref, gg_ref):
    d = x_ref.shape[-1]
    nh = q_ref.shape[1]
    dk = d // nh
    xn = _rms(x_ref[0], g_ref[...]).astype(BF16)

    def heads(ref, val):
        for h in range(nh):
            ref[0, h] = val[:, h * dk:(h + 1) * dk].astype(ref.dtype)

    heads(q_ref, _dot(xn, w_ref[:, 0:d]))
    heads(v_ref, _dot(xn, w_ref[:, 3 * d:4 * d]))
    gg_ref[0] = _dot(xn, w_ref[:, 4 * d:5 * d]).astype(BF16)
    for di, (k_ref, lf_ref) in enumerate(((kf_ref, lff_ref), (kb_ref, lfb_ref))):
        a = _dot(xn, w_ref[:, (1 + di) * d:(2 + di) * d]) + bf_ref[di:di + 1, :]
        lb = lb_ref[di:di + 1, :]
        e = jnp.exp(-jnp.abs(a))
        inv = 1.0 / (1.0 + e)
        pos = a >= 0.0
        sig = jnp.where(pos, inv, e * inv)
        nsig = jnp.where(pos, e * inv, inv)
        heads(lf_ref, jnp.log(lb + (1.0 - lb) * sig) * LOG2_E)
        heads(k_ref, (1.0 - lb) * nsig)


def _hpre(x, g, w_in, b_f, lb):
    b, s, d = x.shape
    nh = H_HEADS
    dk = d // nh
    tm = TOKEN_TILE
    hm = pl.BlockSpec((1, nh, tm, dk), lambda bi, i: (bi, 0, i, 0))
    tile = pl.BlockSpec((1, tm, d), lambda bi, i: (bi, i, 0))
    hshape = lambda dt: jax.ShapeDtypeStruct((b, nh, s, dk), dt)
    return pl.pallas_call(
        _hpre_body,
        grid=(b, s // tm),
        in_specs=[tile, _const_spec((1, d)), _const_spec(w_in.shape), _const_spec((2, d)), _const_spec((2, d))],
        out_specs=[hm, hm, hm, hm, hm, hm, tile],
        out_shape=[hshape(BF16)] * 4 + [hshape(F32)] * 2 + [jax.ShapeDtypeStruct((b, s, d), BF16)],
        compiler_params=_params(2),
        name="hgrn_pre",
    )(x, g.reshape(1, d), w_in, b_f, lb)


def _hgrn_body(rev, q_ref, k_ref, v_ref, lf_ref, o_ref, st_ref):
    nh, ln, dk = q_ref.shape[1], q_ref.shape[2], q_ref.shape[3]
    hb = ln // 2
    sub = 8
    ng = ln // sub
    qside = 0 if rev else 1

    def zero_state(h, carry):
        st_ref[h] = jnp.zeros(st_ref.shape[1:], F32)
        return carry

    lax.fori_loop(0, jnp.where(pl.program_id(1) == 0, nh, 0), zero_state, 0)

    r8 = lax.broadcasted_iota(jnp.int32, (ng, sub, dk), 1)
    tt = lax.broadcasted_iota(jnp.int32, (hb, hb), 0)
    ss = lax.broadcasted_iota(jnp.int32, (hb, hb), 1)
    lvl = (lax.bitcast_convert_type((tt ^ ss).astype(F32), jnp.int32) >> 23) - 127
    lvl = jnp.where((ss > tt) if rev else (ss < tt), lvl, jnp.where(tt == ss, -1, -2))
    halves = (slice(0, hb), slice(hb, ln))
    bcast = lambda t, i: jnp.broadcast_to(t[:, i:i + 1, :], t.shape)

    def head(h):
        lf = lf_ref[0, h]
        q, k, v = q_ref[0, h], k_ref[0, h], v_ref[0, h]
        qf, kf = q.astype(F32), k.astype(F32)
        z_up, z_lo = (kf, qf) if rev else (qf, kf)
        a = [jnp.where(lvl == -1, _dot_nt(q[hv], k[hv]), 0.0) for hv in halves]

        lf3 = lf.reshape(ng, sub, dk)
        z3_up, z3_lo = z_up.reshape(ng, sub, dk), z_lo.reshape(ng, sub, dk)
        cc = lf3
        for s in (1, 2, 4):
            if rev:
                cc = cc + jnp.where(r8 < sub - s, pltpu.roll(cc, sub - s, 1), 0.0)
            else:
                cc = cc + jnp.where(r8 >= s, pltpu.roll(cc, s, 1), 0.0)
        for j, m in enumerate((1, 2, 4)):
            upper = (r8 & m) != 0
            ref = m if rev else m - 1
            if m == 1:
                e = jnp.where(upper != rev, lf3, 0.0)
            elif m == 2:
                e = -jnp.abs(cc - jnp.where(r8 < 4, bcast(cc, ref), bcast(cc, ref + 4)))
            else:
                e = -jnp.abs(cc - bcast(cc, ref))
            w = (jnp.where(upper, z3_up, z3_lo) * jnp.exp2(e)).reshape(ln, dk).astype(BF16)
            for i, hv in enumerate(halves):
                a[i] = jnp.where(lvl == j, _dot_nt(w[hv], w[hv]), a[i])
        if rev:
            y = cc
            tot = bcast(cc, 0)
            x = tot - cc
        else:
            x = cc
            tot = bcast(cc, sub - 1)
            y = tot - cc
        x, y, tot = x.reshape(ln, dk), y.reshape(ln, dk), tot.reshape(ln, dk)

        m, j = sub, 3
        while m < ln:
            nb = ln // (2 * m)
            split = lambda t: t.reshape(nb, 2, m, t.shape[-1])
            join = lambda lo, up: jnp.stack([lo, up], axis=1).reshape(ln, dk)
            x4, y4, t4 = split(x), split(y), split(tot)
            w_lo = split(z_lo)[:, 0] * jnp.exp2(y4[:, 0])
            w_up = split(z_up)[:, 1] * jnp.exp2(x4[:, 1])
            if m < hb:
                w = join(w_lo, w_up).astype(BF16)
                wq = (w_lo if rev else w_up).astype(BF16)
                nbh = nb // 2
                mask = lvl.reshape(nbh, 2, m, hb)[:, qside] == j
                for i, hv in enumerate(halves):
                    p = _dot_nt(wq[i * nbh:(i + 1) * nbh].reshape(nbh * m, dk), w[hv])
                    a4 = a[i].reshape(nbh, 2, m, hb)
                    aq = jnp.where(mask, p.reshape(nbh, m, hb), a4[:, qside])
                    parts = [aq, a4[:, 1]] if rev else [a4[:, 0], aq]
                    a[i] = jnp.stack(parts, axis=1).reshape(hb, hb)
            else:
                wl, wu = w_lo.reshape(hb, dk).astype(BF16), w_up.reshape(hb, dk).astype(BF16)
                cross = _dot_nt(wl, wu) if rev else _dot_nt(wu, wl)
            x = join(x4[:, 0], x4[:, 1] + t4[:, 0])
            y = join(y4[:, 0] + t4[:, 1], y4[:, 1])
            tsum = t4[:, 0] + t4[:, 1]
            tot = join(tsum, tsum)
            m, j = 2 * m, j + 1
        e_q, e_k = (y, x) if rev else (x, y)
        st = st_ref[h]
        inter = _dot_nt((qf * jnp.exp2(e_q)).astype(BF16), st.astype(BF16))
        a0, a1, cr = a[0].astype(BF16), a[1].astype(BF16), cross.astype(BF16)
        if rev:
            o0 = _dot(jnp.concatenate([a0, cr], axis=1), v)
            o1 = _dot(a1, v[halves[1]])
        else:
            o0 = _dot(a0, v[halves[0]])
            o1 = _dot(jnp.concatenate([cr, a1], axis=1), v)
        o_ref[0, h, halves[0], :] = o0 + inter[halves[0]]
        o_ref[0, h, halves[1], :] = o1 + inter[halves[1]]
        kw = (kf * jnp.exp2(e_k)).astype(BF16)
        st_ref[h] = jnp.exp2(tot[0:1, :]) * st + _dot_tn(v, kw)

    def group(i, carry):
        for u in range(H_UNROLL):
            head(H_UNROLL * i + u)
        return carry

    lax.fori_loop(0, nh // H_UNROLL, group, 0)


def _hgrn_scan(rev, q, k, v, lf):
    b, nh, s, dk = q.shape
    ln = SCAN_CHUNK
    nc = s // ln
    cidx = (lambda i: nc - 1 - i) if rev else (lambda i: i)
    hm = pl.BlockSpec((1, nh, ln, dk), lambda bi, i: (bi, 0, cidx(i), 0))
    return pl.pallas_call(
        functools.partial(_hgrn_body, rev),
        grid=(b, nc),
        in_specs=[hm, hm, hm, hm],
        out_specs=hm,
        out_shape=jax.ShapeDtypeStruct((b, nh, s, dk), F32),
        scratch_shapes=[pltpu.VMEM((nh, dk, dk), F32)],
        compiler_params=_params(2),
        name="hgrn_scan_bwd" if rev else "hgrn_scan_fwd",
    )(q, k, v, lf)


def _hpost_body(of_ref, ob_ref, gg_ref, x_ref, hn_ref, wo_ref, o_ref):
    nh, dk = of_ref.shape[1], of_ref.shape[3]
    parts = []
    for h in range(nh):
        o = of_ref[0, h] + ob_ref[0, h]
        o = o * lax.rsqrt(jnp.mean(o * o, axis=-1, keepdims=True) + EPS) * hn_ref[:, h * dk:(h + 1) * dk]
        parts.append(o)
    o = jnp.concatenate(parts, axis=-1) * _sigmoid(gg_ref[0].astype(F32))
    o_ref[0] = x_ref[0] + _dot(o.astype(BF16), wo_ref[...])


def _hpost(o_f, o_b, gg, x, head_norm, w_out):
    b, nh, s, dk = o_f.shape
    d = x.shape[-1]
    tm = TOKEN_TILE
    hm = pl.BlockSpec((1, nh, tm, dk), lambda bi, i: (bi, 0, i, 0))
    tile = pl.BlockSpec((1, tm, d), lambda bi, i: (bi, i, 0))
    return pl.pallas_call(
        _hpost_body,
        grid=(b, s // tm),
        in_specs=[hm, hm, tile, tile, _const_spec((1, d)), _const_spec(w_out.shape)],
        out_specs=tile,
        out_shape=jax.ShapeDtypeStruct((b, s, d), F32),
        compiler_params=_params(2),
        name="hgrn_post",
    )(o_f, o_b, gg, x, head_norm.reshape(1, d), w_out)


def _lower_bounds(logits):
    pr = jax.nn.softmax(logits.astype(F32), axis=0)
    return jnp.cumsum(pr, axis=0) - pr[0]


def _hgrn_mixer(x, mix_g, w_in, b_f, lb, head_norm, w_out):
    q, k_f, k_b, v, lf_f, lf_b, gg = _hpre(x, mix_g, w_in.astype(BF16), b_f, lb)
    o_f = _hgrn_scan(False, q, k_f, v, lf_f)
    o_b = _hgrn_scan(True, q, k_b, v, lf_b)
    return _hpost(o_f, o_b, gg, x, head_norm, w_out.astype(BF16))


def _trunk(x, p, ffn1_norm, ffn1_w_in, ffn1_w_out, mix_norm,
           m_w_in, m_conv_w, m_conv_b, m_wq, m_wk, m_wv, m_w_gate, m_b_gate, m_head_norm, m_skip, m_w_out,
           h_w_in, h_b_f, h_lb_logits, h_head_norm, h_w_out,
           ffn2_norm, ffn2_w_in, ffn2_w_out, ple_norm, ple_w_gate, ple_w_proj, final_norm):
    b, s, d = x.shape
    depth = ffn1_norm.shape[0]
    n_mixers = 2
    lb_fwd = _lower_bounds(h_lb_logits[0])
    lb_bwd = _lower_bounds(h_lb_logits[1])
    flat = lambda t: t.reshape(b * s, t.shape[-1])
    for i in range(depth):
        x = _ffn(flat(x), ffn1_norm[i], ffn1_w_in[i].astype(BF16), ffn1_w_out[i].astype(BF16)).reshape(b, s, d)
        j = i // n_mixers
        if i % n_mixers == 0:
            x = _mlstm_mixer(x, mix_norm[i], m_w_in[j], m_conv_w[j], m_conv_b[j], m_wq[j], m_wk[j], m_wv[j],
                             m_w_gate[j], m_b_gate[j], m_head_norm[j], m_skip[j], m_w_out[j])
        else:
            lb = jnp.stack([lb_fwd[i], lb_bwd[i]])
            x = _hgrn_mixer(x, mix_norm[i], h_w_in[j], h_b_f[j], lb, h_head_norm[j], h_w_out[j])
        ple = (flat(p[i]), ple_norm[i], ple_w_gate[i].astype(BF16), ple_w_proj[i].astype(BF16))
        x = _ffn(flat(x), ffn2_norm[i], ffn2_w_in[i].astype(BF16), ffn2_w_out[i].astype(BF16), ple=ple,
                 final_g=final_norm if i == depth - 1 else None).reshape(b, s, d)
    return x


def kernel(x_prompt, x_sample, p_prompt, p_sample, ffn1_norm, ffn1_w_in, ffn1_w_out, mix_norm, m_w_in, m_conv_w, m_conv_b, m_wq, m_wk, m_wv, m_w_gate, m_b_gate, m_head_norm, m_skip, m_w_out, h_w_in, h_b_f, h_lb_logits, h_head_norm, h_w_out, ffn2_norm, ffn2_w_in, ffn2_w_out, ple_norm, ple_w_gate, ple_w_proj, final_norm):
    weights = (ffn1_norm, ffn1_w_in, ffn1_w_out, mix_norm,
               m_w_in, m_conv_w, m_conv_b, m_wq, m_wk, m_wv, m_w_gate, m_b_gate, m_head_norm, m_skip, m_w_out,
               h_w_in, h_b_f, h_lb_logits, h_head_norm, h_w_out,
               ffn2_norm, ffn2_w_in, ffn2_w_out, ple_norm, ple_w_gate, ple_w_proj, final_norm)
    return (_trunk(x_prompt, p_prompt, *weights), _trunk(x_sample, p_sample, *weights))
```

```python
import functools

import jax
import jax.numpy as jnp
from jax import lax
from jax.experimental import pallas as pl
from jax.experimental.pallas import tpu as pltpu

F32 = jnp.float32
BF16 = jnp.bfloat16
EPS = 1e-6
LOG2_E = 1.4426950408889634

M_HEADS = 4
M_QKV_BLOCK = 4
M_CONV = 5
H_HEADS = 8
PLE_GROUP = 256
TOKEN_TILE = 512
SCAN_CHUNK = 256
CONV_HALO = 16
GATE_LANES = 128
VMEM_LIMIT = 56 * 1024 * 1024


def _params(n_axes):
    return pltpu.CompilerParams(dimension_semantics=("arbitrary",) * n_axes,
                                vmem_limit_bytes=VMEM_LIMIT)


def _const_spec(shape):
    nd = len(shape)
    return pl.BlockSpec(shape, lambda *_: (0,) * nd, pipeline_mode=pl.Buffered(1))


def _rms(x, g):
    return x * lax.rsqrt(jnp.mean(x * x, axis=-1, keepdims=True) + EPS) * g


def _sigmoid(x):
    return 1.0 / (1.0 + jnp.exp(-x))


def _dot(a, b):
    return jnp.dot(a, b, preferred_element_type=F32)


def _dot_nt(a, b):
    return lax.dot_general(a, b, (((1,), (1,)), ((), ())), preferred_element_type=F32)


def _dot_tn(a, b):
    return lax.dot_general(a, b, (((0,), (0,)), ((), ())), preferred_element_type=F32)


def _mix_out(n_heads, ha_ref, hb_ref, hn_ref, add_ref, addw_ref, gate_ref, wo_ref, gate_fn):
    dh = ha_ref.shape[-1] // n_heads
    parts = []
    for h in range(n_heads):
        hs = slice(h * dh, (h + 1) * dh)
        hsum = ha_ref[:, hs].astype(F32) + hb_ref[:, hs].astype(F32)
        y = hsum * lax.rsqrt(jnp.mean(hsum * hsum, axis=-1, keepdims=True) + EPS) * hn_ref[:, hs]
        if add_ref is not None:
            y = y + addw_ref[:, hs] * add_ref[:, hs].astype(F32)
        parts.append((y * gate_fn(gate_ref[:, hs].astype(F32))).astype(BF16))
    return _dot(jnp.concatenate(parts, axis=-1), wo_ref[...])


def _ffn_body(mix, has_ple, has_final, *refs):
    it = iter(refs)
    x = next(it)[...]
    if mix == "mlstm":
        ha, hb, hn, xc, sk, z, wo = (next(it) for _ in range(7))
        x = x + _mix_out(M_HEADS, ha, hb, hn, xc, sk, z, wo, lambda t: t * _sigmoid(t))
    elif mix == "hgrn":
        ha, hb, hn, gg, wo = (next(it) for _ in range(5))
        x = x + _mix_out(H_HEADS, ha, hb, hn, None, None, gg, wo, _sigmoid)
    g_ref, win_ref, wout_ref = (next(it) for _ in range(3))
    if has_ple:
        p_ref, pg_ref, pwg_ref, pwp_ref = (next(it) for _ in range(4))
    if has_final:
        fg_ref = next(it)
    o_ref = next(it)
    d_ff = wout_ref.shape[0]
    xn = _rms(x, g_ref[...]).astype(BF16)
    a = _dot(xn, win_ref[:, :d_ff])
    u = _dot(xn, win_ref[:, d_ff:])
    act = (a * _sigmoid(a) * u).astype(BF16)
    x = x + 0.5 * _dot(act, wout_ref[...])
    if has_ple:
        xg = _rms(x, pg_ref[...]).astype(BF16)
        gate = _sigmoid(_dot(xg, pwg_ref[...]))
        x = x + gate * _dot(p_ref[...].astype(BF16), pwp_ref[...])
    if has_final:
        x = _rms(x, fg_ref[...])
    o_ref[...] = x


def _ffn(x, g, w_in, w_out, mix=None, ple=None, final_g=None):
    t, d = x.shape
    tm = TOKEN_TILE if mix is None else TOKEN_TILE // 2
    row = lambda w: pl.BlockSpec((tm, w), lambda i: (i, 0))
    args, specs = [x], [row(d)]
    kind = None
    if mix is not None:
        kind, operands, per_token = mix
        for item, tok in zip(operands, per_token):
            if item.ndim == 1:
                item = item.reshape(1, -1)
            args.append(item)
            specs.append(row(item.shape[1]) if tok else _const_spec(item.shape))
    args += [g.reshape(1, d), w_in, w_out]
    specs += [_const_spec((1, d)), _const_spec(w_in.shape), _const_spec(w_out.shape)]
    if ple is not None:
        p, pg, pwg, pwp = ple
        args += [p, pg.reshape(1, d), pwg, pwp]
        specs += [row(p.shape[1]), _const_spec((1, d)), _const_spec(pwg.shape), _const_spec(pwp.shape)]
    if final_g is not None:
        args.append(final_g.reshape(1, d))
        specs.append(_const_spec((1, d)))
    return pl.pallas_call(
        functools.partial(_ffn_body, kind, ple is not None, final_g is not None),
        grid=(t // tm,),
        in_specs=specs,
        out_specs=row(d),
        out_shape=jax.ShapeDtypeStruct((t, d), F32),
        compiler_params=_params(1),
        name="ffn" if mix is None else "mix_out_ffn",
    )(*args)


def _mpre_body(x_ref, g_ref, w_ref, xm_ref, z_ref):
    n = xm_ref.shape[-1]
    xn = _rms(x_ref[...], g_ref[...]).astype(BF16)
    xm_ref[...] = _dot(xn, w_ref[:, :n]).astype(BF16)
    z_ref[...] = _dot(xn, w_ref[:, n:]).astype(BF16)


def _mpre(x, g, w_in):
    t, d = x.shape
    n = w_in.shape[1] // 2
    tm = TOKEN_TILE
    row = lambda w: pl.BlockSpec((tm, w), lambda i: (i, 0))
    return pl.pallas_call(
        _mpre_body,
        grid=(t // tm,),
        in_specs=[row(d), _const_spec((1, d)), _const_spec(w_in.shape)],
        out_specs=[row(n), row(n)],
        out_shape=[jax.ShapeDtypeStruct((t, n), BF16)] * 2,
        compiler_params=_params(1),
        name="mlstm_pre",
    )(x, g.reshape(1, d), w_in)


def _log_sigmoid(x):
    return jnp.minimum(x, 0.0) - jnp.log1p(jnp.exp(-jnp.abs(x)))


def _mconv_body(k_scale, xm_ref, prev_ref, next_ref, cw_ref, cb_ref, wqk_ref, wv_ref, wg_ref, bg_ref,
                q_ref, k_ref, v_ref, xc_ref, g_ref, gt_ref, ext_ref):
    i = pl.program_id(1)
    last = pl.num_programs(1) - 1
    ts, c = xm_ref.shape[1], xm_ref.shape[2]
    hl = CONV_HALO
    pad = M_CONV // 2
    xm = xm_ref[0]
    ext_ref[0:hl, :] = jnp.where(i > 0, prev_ref[0].astype(F32), 0.0)
    ext_ref[hl:hl + ts, :] = xm.astype(F32)
    ext_ref[hl + ts:hl + ts + hl, :] = jnp.where(i < last, next_ref[0].astype(F32), 0.0)
    acc = jnp.broadcast_to(cb_ref[...], (ts, c))
    for j in range(M_CONV):
        acc = acc + cw_ref[j:j + 1, :] * ext_ref[hl - pad + j:hl - pad + j + ts, :]
    xc = (acc * _sigmoid(acc)).astype(BF16)
    xc_ref[0] = xc
    gw = PLE_GROUP
    gates = jnp.broadcast_to(bg_ref[...], (ts, GATE_LANES))
    for j in range(c // gw):
        cs = slice(j * gw, (j + 1) * gw)
        qk = _dot(xc[:, cs], wqk_ref[j])
        qj = qk[:, :gw].astype(BF16)
        kj = qk[:, gw:]
        kjb = kj.astype(BF16)
        vj = _dot(xm[:, cs], wv_ref[j]).astype(BF16)
        q_ref[0, :, cs] = qj
        k_ref[0, :, cs] = (kj * k_scale).astype(BF16)
        v_ref[0, :, cs] = vj
        gates = gates + _dot(qj, wg_ref[0, cs, :]) + _dot(kjb, wg_ref[1, cs, :]) + _dot(vj, wg_ref[2, cs, :])
    col = lax.broadcasted_iota(jnp.int32, gates.shape, 1)
    is_forget = (col % (2 * M_HEADS)) >= M_HEADS
    gates = jnp.where(is_forget, _log_sigmoid(gates), gates)
    g_ref[0] = gates
    gt_ref[0] = gates.T[:4 * M_HEADS, :]


def _mconv(xm, conv_w, conv_b, wqk, wv, wg, bg, k_scale):
    b, s, c = xm.shape
    ts = TOKEN_TILE
    hb = ts // CONV_HALO
    nh_blocks = s // CONV_HALO
    tile = lambda w, dt=None: pl.BlockSpec((1, ts, w), lambda bi, i: (bi, i, 0))
    return pl.pallas_call(
        functools.partial(_mconv_body, k_scale),
        grid=(b, s // ts),
        in_specs=[
            tile(c),
            pl.BlockSpec((1, CONV_HALO, c), lambda bi, i: (bi, jnp.maximum(i * hb - 1, 0), 0)),
            pl.BlockSpec((1, CONV_HALO, c), lambda bi, i: (bi, jnp.minimum((i + 1) * hb, nh_blocks - 1), 0)),
            _const_spec(conv_w.shape), _const_spec((1, c)),
            _const_spec(wqk.shape), _const_spec(wv.shape), _const_spec(wg.shape), _const_spec(bg.shape),
        ],
        out_specs=[tile(c), tile(c), tile(c), tile(c), tile(GATE_LANES),
                   pl.BlockSpec((1, 4 * M_HEADS, ts), lambda bi, i: (bi, 0, i))],
        out_shape=[jax.ShapeDtypeStruct((b, s, c), BF16)] * 4
        + [jax.ShapeDtypeStruct((b, s, GATE_LANES), F32), jax.ShapeDtypeStruct((b, 4 * M_HEADS, s), F32)],
        scratch_shapes=[pltpu.VMEM((ts + 2 * CONV_HALO, c), F32)],
        compiler_params=_params(2),
        name="mlstm_conv_qkv",
    )(xm, xm, xm, conv_w, conv_b.reshape(1, c), wqk, wv, wg, bg)


def _running(x, axis, rev, op, fill):
    n = x.shape[axis]
    idx = lax.broadcasted_iota(jnp.int32, x.shape, axis)
    s = 1
    while s < n:
        if rev:
            x = op(x, jnp.where(idx < n - s, pltpu.roll(x, n - s, axis), fill))
        else:
            x = op(x, jnp.where(idx >= s, pltpu.roll(x, s, axis), fill))
        s *= 2
    return x


def _zero_rows_at_first_chunk(refs, rows_per_trip):
    rows = refs[0].shape[0]

    def zero(i, carry):
        r0 = pl.multiple_of(i * rows_per_trip, rows_per_trip)
        for ref in refs:
            ref[pl.ds(r0, rows_per_trip), :] = jnp.zeros((rows_per_trip, ref.shape[1]), F32)
        return carry

    lax.fori_loop(0, jnp.where(pl.program_id(1) == 0, rows // rows_per_trip, 0), zero, 0)


def _mlstm_body(rev, q_ref, k_ref, v_ref, g_ref, gt_ref, o_ref, *state_refs):
    nh = M_HEADS
    c_refs, n_refs, m_refs = state_refs[:nh], state_refs[nh:2 * nh], state_refs[2 * nh:]
    ln = q_ref.shape[1]
    dh = q_ref.shape[2] // nh
    _zero_rows_at_first_chunk(c_refs, 64)
    _zero_rows_at_first_chunk(n_refs + m_refs, 8)

    g = g_ref[0] * LOG2_E
    gt = gt_ref[0] * LOG2_E
    bcol = _running(g, 0, rev, jnp.add, 0.0)
    brow = _running(gt, 1, rev, jnp.add, 0.0)
    gdcol = g - pltpu.roll(bcol, g.shape[1] - nh, 1)
    gdrow = gt - pltpu.roll(brow, gt.shape[0] - nh, 0)
    mcol = _running(gdcol, 0, rev, jnp.maximum, -jnp.inf)
    tt = lax.broadcasted_iota(jnp.int32, (ln, ln), 0)
    ss = lax.broadcasted_iota(jnp.int32, (ln, ln), 1)
    valid = (ss >= tt) if rev else (ss <= tt)
    end = 0 if rev else ln - 1
    goff = 2 * nh if rev else 0
    for h in range(nh):
        hs = slice(h * dh, (h + 1) * dh)
        ci, cf = goff + h, goff + nh + h
        gd_row, gd_col = gdrow[ci:ci + 1, :], gdcol[:, ci:ci + 1]
        b_col = bcol[:, cf:cf + 1]
        b_all = b_col[end:end + 1, :]
        m_prev = m_refs[h][0:1, 0:1]
        big_m = jnp.maximum(mcol[:, ci:ci + 1], m_prev)
        m_last = big_m[end:end + 1, :]
        q, k, v = q_ref[0, :, hs], k_ref[0, :, hs], v_ref[0, :, hs]

        s_mat = _dot_nt(q, k) * jnp.exp2(jnp.where(valid, gd_row - big_m, -jnp.inf))
        w_inter = jnp.exp2(m_prev - big_m)
        cst = c_refs[h][...]
        nst = n_refs[h][0:1, :]
        num = _dot(s_mat.astype(BF16), v) + w_inter * _dot(q, cst.astype(BF16))
        den = jnp.sum(s_mat, axis=-1, keepdims=True) + \
            w_inter * jnp.sum(q.astype(F32) * nst, axis=-1, keepdims=True)
        hh = num * (1.0 / jnp.maximum(jnp.abs(den), jnp.exp2(-(b_col + big_m))))
        o_ref[0, :, hs] = hh.astype(o_ref.dtype)

        wk_col = jnp.exp2(gd_col - m_last)
        w_old = jnp.exp2(m_prev - m_last)
        kw = k.astype(F32) * wk_col
        c_refs[h][...] = w_old * cst + _dot_tn(kw.astype(BF16), v)
        n_refs[h][...] = jnp.broadcast_to(w_old * nst + jnp.sum(kw, axis=0, keepdims=True), n_refs[h].shape)
        m_refs[h][...] = jnp.broadcast_to(b_all + m_last, m_refs[h].shape)


def _mlstm_scan(rev, q, k, v, g, gt):
    b, s, c = q.shape
    ln = SCAN_CHUNK
    nc = s // ln
    dh = c // M_HEADS
    cidx = (lambda i: nc - 1 - i) if rev else (lambda i: i)
    tile = lambda w: pl.BlockSpec((1, ln, w), lambda bi, i: (bi, cidx(i), 0))
    return pl.pallas_call(
        functools.partial(_mlstm_body, rev),
        grid=(b, nc),
        in_specs=[tile(c), tile(c), tile(c), tile(GATE_LANES),
                  pl.BlockSpec((1, 4 * M_HEADS, ln), lambda bi, i: (bi, 0, cidx(i)))],
        out_specs=tile(c),
        out_shape=jax.ShapeDtypeStruct((b, s, c), BF16),
        scratch_shapes=[pltpu.VMEM((dh, dh), F32)] * M_HEADS + [pltpu.VMEM((8, dh), F32)] * M_HEADS
        + [pltpu.VMEM((8, 128), F32)] * M_HEADS,
        compiler_params=_params(2),
        name="mlstm_scan_bwd" if rev else "mlstm_scan_fwd",
    )(q, k, v, g, gt)


def _blockdiag_tiles(w):
    nb, bc, bd = w.shape
    per = PLE_GROUP // bc
    w = w.reshape(nb // per, per, bc, bd)
    eye = jnp.eye(per, dtype=w.dtype)
    dense = jnp.einsum('jncd,nm->jncmd', w, eye)
    return dense.reshape(nb // per, per * bc, per * bd)


def _mlstm_mixer(x, mix_g, w_in, conv_w, conv_b, wq, wk, wv, w_gate, b_gate, head_norm, skip, w_out):
    b, s, d = x.shape
    c = w_out.shape[0]
    xm, z = _mpre(x.reshape(b * s, d), mix_g, w_in.astype(BF16))
    xm = xm.reshape(b, s, c)
    wqk = jnp.concatenate([_blockdiag_tiles(wq), _blockdiag_tiles(wk)], axis=-1).astype(BF16)
    wvt = _blockdiag_tiles(wv).astype(BF16)
    ng = w_gate.shape[1]
    wg = jnp.pad(w_gate.reshape(3, c, ng), ((0, 0), (0, 0), (0, GATE_LANES - ng))).astype(BF16)
    bg = jnp.pad(b_gate.reshape(1, ng), ((0, 0), (0, GATE_LANES - ng)))
    k_scale = float(c // M_HEADS) ** -0.5
    q, k, v, xc, g, gt = _mconv(xm, conv_w, conv_b, wqk, wvt, wg, bg, k_scale)
    flat = lambda t: t.reshape(b * s, c)
    h_f = _mlstm_scan(False, q, k, v, g, gt)
    h_b = _mlstm_scan(True, q, k, v, g, gt)
    return ("mlstm", (flat(h_f), flat(h_b), head_norm, flat(xc), skip, z, w_out.astype(BF16)),
            (True, True, False, True, False, True, False))


def _hpre_body(x_ref, g_ref, w_ref, bf_ref, lb_ref, q_ref, kf_ref, kb_ref, v_ref, lff_ref, lfb_ref, gg_ref):
    d = x_ref.shape[-1]
    xn = _rms(x_ref[...], g_ref[...]).astype(BF16)
    proj = lambda col: _dot(xn, w_ref[:, col * d:(col + 1) * d])
    q_ref[...] = proj(0).astype(BF16)
    v_ref[...] = proj(3).astype(BF16)
    gg_ref[...] = proj(4).astype(BF16)
    for di, (k_ref, lf_ref) in enumerate(((kf_ref, lff_ref), (kb_ref, lfb_ref))):
        a = proj(1 + di) + bf_ref[di:di + 1, :]
        lb = lb_ref[di:di + 1, :]
        e = jnp.exp(-jnp.abs(a))
        inv = 1.0 / (1.0 + e)
        pos = a >= 0.0
        sig = jnp.where(pos, inv, e * inv)
        nsig = jnp.where(pos, e * inv, inv)
        lf_ref[...] = jnp.log(lb + (1.0 - lb) * sig) * LOG2_E
        k_ref[...] = ((1.0 - lb) * nsig).astype(BF16)


def _hpre(x, g, w_in, b_f, lb):
    t, d = x.shape
    tm = TOKEN_TILE
    row = pl.BlockSpec((tm, d), lambda i: (i, 0))
    shape = lambda dt: jax.ShapeDtypeStruct((t, d), dt)
    return pl.pallas_call(
        _hpre_body,
        grid=(t // tm,),
        in_specs=[row, _const_spec((1, d)), _const_spec(w_in.shape), _const_spec((2, d)), _const_spec((2, d))],
        out_specs=[row] * 7,
        out_shape=[shape(BF16)] * 4 + [shape(F32)] * 2 + [shape(BF16)],
        compiler_params=_params(1),
        name="hgrn_pre",
    )(x, g.reshape(1, d), w_in, b_f, lb)


def _hgrn_body(rev, q_ref, k_ref, v_ref, lf_ref, o_ref, *st_refs):
    nh = len(st_refs)
    ln = q_ref.shape[1]
    dk = q_ref.shape[2] // nh
    hb = ln // 2
    sub = 8
    ng = ln // sub
    qside = 0 if rev else 1
    _zero_rows_at_first_chunk(st_refs, 32)

    r8 = lax.broadcasted_iota(jnp.int32, (ng, sub, dk), 1)
    tt = lax.broadcasted_iota(jnp.int32, (hb, hb), 0)
    ss = lax.broadcasted_iota(jnp.int32, (hb, hb), 1)
    lvl = (lax.bitcast_convert_type((tt ^ ss).astype(F32), jnp.int32) >> 23) - 127
    lvl = jnp.where((ss > tt) if rev else (ss < tt), lvl, jnp.where(tt == ss, -1, -2))
    halves = (slice(0, hb), slice(hb, ln))
    bcast = lambda t, i: jnp.broadcast_to(t[:, i:i + 1, :], t.shape)

    def head(h):
        hs = slice(h * dk, (h + 1) * dk)
        lf = lf_ref[0, :, hs]
        q, k, v = q_ref[0, :, hs], k_ref[0, :, hs], v_ref[0, :, hs]
        qf, kf = q.astype(F32), k.astype(F32)
        z_up, z_lo = (kf, qf) if rev else (qf, kf)
        a = [jnp.where(lvl == -1, _dot_nt(q[hv], k[hv]), 0.0) for hv in halves]

        lf3 = lf.reshape(ng, sub, dk)
        z3_up, z3_lo = z_up.reshape(ng, sub, dk), z_lo.reshape(ng, sub, dk)
        cc = lf3
        for s in (1, 2, 4):
            if rev:
                cc = cc + jnp.where(r8 < sub - s, pltpu.roll(cc, sub - s, 1), 0.0)
            else:
                cc = cc + jnp.where(r8 >= s, pltpu.roll(cc, s, 1), 0.0)
        for j, m in enumerate((1, 2, 4)):
            upper = (r8 & m) != 0
            ref = m if rev else m - 1
            if m == 1:
                e = jnp.where(upper != rev, lf3, 0.0)
            elif m == 2:
                e = -jnp.abs(cc - jnp.where(r8 < 4, bcast(cc, ref), bcast(cc, ref + 4)))
            else:
                e = -jnp.abs(cc - bcast(cc, ref))
            w = (jnp.where(upper, z3_up, z3_lo) * jnp.exp2(e)).reshape(ln, dk).astype(BF16)
            for i, hv in enumerate(halves):
                a[i] = jnp.where(lvl == j, _dot_nt(w[hv], w[hv]), a[i])
        if rev:
            y = cc
            tot = bcast(cc, 0)
            x = tot - cc
        else:
            x = cc
            tot = bcast(cc, sub - 1)
            y = tot - cc
        x, y, tot = x.reshape(ln, dk), y.reshape(ln, dk), tot.reshape(ln, dk)

        m, j = sub, 3
        while m < ln:
            nb = ln // (2 * m)
            split = lambda t: t.reshape(nb, 2, m, t.shape[-1])
            join = lambda lo, up: jnp.stack([lo, up], axis=1).reshape(ln, dk)
            x4, y4, t4 = split(x), split(y), split(tot)
            w_lo = split(z_lo)[:, 0] * jnp.exp2(y4[:, 0])
            w_up = split(z_up)[:, 1] * jnp.exp2(x4[:, 1])
            if m < hb:
                w = join(w_lo, w_up).astype(BF16)
                wq = (w_lo if rev else w_up).astype(BF16)
                nbh = nb // 2
                mask = lvl.reshape(nbh, 2, m, hb)[:, qside] == j
                for i, hv in enumerate(halves):
                    p = _dot_nt(wq[i * nbh:(i + 1) * nbh].reshape(nbh * m, dk), w[hv])
                    a4 = a[i].reshape(nbh, 2, m, hb)
                    aq = jnp.where(mask, p.reshape(nbh, m, hb), a4[:, qside])
                    parts = [aq, a4[:, 1]] if rev else [a4[:, 0], aq]
                    a[i] = jnp.stack(parts, axis=1).reshape(hb, hb)
            else:
                wl, wu = w_lo.reshape(hb, dk).astype(BF16), w_up.reshape(hb, dk).astype(BF16)
                cross = _dot_nt(wl, wu) if rev else _dot_nt(wu, wl)
            x = join(x4[:, 0], x4[:, 1] + t4[:, 0])
            y = join(y4[:, 0] + t4[:, 1], y4[:, 1])
            tsum = t4[:, 0] + t4[:, 1]
            tot = join(tsum, tsum)
            m, j = 2 * m, j + 1
        e_q, e_k = (y, x) if rev else (x, y)
        st = st_refs[h][...]
        inter = _dot_nt((qf * jnp.exp2(e_q)).astype(BF16), st.astype(BF16))
        a0, a1, cr = a[0].astype(BF16), a[1].astype(BF16), cross.astype(BF16)
        if rev:
            o0 = _dot(jnp.concatenate([a0, cr], axis=1), v)
            o1 = _dot(a1, v[halves[1]])
        else:
            o0 = _dot(a0, v[halves[0]])
            o1 = _dot(jnp.concatenate([cr, a1], axis=1), v)
        o_ref[0, halves[0], hs] = (o0 + inter[halves[0]]).astype(o_ref.dtype)
        o_ref[0, halves[1], hs] = (o1 + inter[halves[1]]).astype(o_ref.dtype)
        kw = (kf * jnp.exp2(e_k)).astype(BF16)
        st_refs[h][...] = jnp.exp2(tot[0:1, :]) * st + _dot_tn(v, kw)

    for h in range(nh):
        head(h)


def _hgrn_scan(rev, q, k, v, lf):
    b, s, d = q.shape
    nh = H_HEADS
    dk = d // nh
    ln = SCAN_CHUNK
    nc = s // ln
    cidx = (lambda i: nc - 1 - i) if rev else (lambda i: i)
    tile = pl.BlockSpec((1, ln, d), lambda bi, i: (bi, cidx(i), 0))
    return pl.pallas_call(
        functools.partial(_hgrn_body, rev),
        grid=(b, nc),
        in_specs=[tile] * 4,
        out_specs=tile,
        out_shape=jax.ShapeDtypeStruct((b, s, d), BF16),
        scratch_shapes=[pltpu.VMEM((dk, dk), F32)] * nh,
        compiler_params=_params(2),
        name="hgrn_scan_bwd" if rev else "hgrn_scan_fwd",
    )(q, k, v, lf)


def _lower_bounds(logits):
    pr = jax.nn.softmax(logits.astype(F32), axis=0)
    return jnp.cumsum(pr, axis=0) - pr[0]


def _hgrn_mixer(x, mix_g, w_in, b_f, lb, head_norm, w_out):
    b, s, d = x.shape
    q, k_f, k_b, v, lf_f, lf_b, gg = _hpre(x.reshape(b * s, d), mix_g, w_in.astype(BF16), b_f, lb)
    seq = lambda t: t.reshape(b, s, d)
    o_f = _hgrn_scan(False, seq(q), seq(k_f), seq(v), seq(lf_f))
    o_b = _hgrn_scan(True, seq(q), seq(k_b), seq(v), seq(lf_b))
    flat = lambda t: t.reshape(b * s, d)
    return "hgrn", (flat(o_f), flat(o_b), head_norm, gg, w_out.astype(BF16)), (True, True, False, True, False)


def _trunk(x, p, ffn1_norm, ffn1_w_in, ffn1_w_out, mix_norm,
           m_w_in, m_conv_w, m_conv_b, m_wq, m_wk, m_wv, m_w_gate, m_b_gate, m_head_norm, m_skip, m_w_out,
           h_w_in, h_b_f, h_lb_logits, h_head_norm, h_w_out,
           ffn2_norm, ffn2_w_in, ffn2_w_out, ple_norm, ple_w_gate, ple_w_proj, final_norm):
    b, s, d = x.shape
    depth = ffn1_norm.shape[0]
    n_mixers = 2
    lb_fwd = _lower_bounds(h_lb_logits[0])
    lb_bwd = _lower_bounds(h_lb_logits[1])
    x = x.reshape(b * s, d)
    for i in range(depth):
        x = _ffn(x, ffn1_norm[i], ffn1_w_in[i].astype(BF16), ffn1_w_out[i].astype(BF16))
        j = i // n_mixers
        xs = x.reshape(b, s, d)
        if i % n_mixers == 0:
            mix = _mlstm_mixer(xs, mix_norm[i], m_w_in[j], m_conv_w[j], m_conv_b[j], m_wq[j], m_wk[j], m_wv[j],
                               m_w_gate[j], m_b_gate[j], m_head_norm[j], m_skip[j], m_w_out[j])
        else:
            lb = jnp.stack([lb_fwd[i], lb_bwd[i]])
            mix = _hgrn_mixer(xs, mix_norm[i], h_w_in[j], h_b_f[j], lb, h_head_norm[j], h_w_out[j])
        ple = (p[i].reshape(b * s, -1), ple_norm[i], ple_w_gate[i].astype(BF16), ple_w_proj[i].astype(BF16))
        x = _ffn(x, ffn2_norm[i], ffn2_w_in[i].astype(BF16), ffn2_w_out[i].astype(BF16), mix=mix, ple=ple,
                 final_g=final_norm if i == depth - 1 else None)
    return x.reshape(b, s, d)


def kernel(x_prompt, x_sample, p_prompt, p_sample, ffn1_norm, ffn1_w_in, ffn1_w_out, mix_norm, m_w_in, m_conv_w, m_conv_b, m_wq, m_wk, m_wv, m_w_gate, m_b_gate, m_head_norm, m_skip, m_w_out, h_w_in, h_b_f, h_lb_logits, h_head_norm, h_w_out, ffn2_norm, ffn2_w_in, ffn2_w_out, ple_norm, ple_w_gate, ple_w_proj, final_norm):
    weights = (ffn1_norm, ffn1_w_in, ffn1_w_out, mix_norm,
               m_w_in, m_conv_w, m_conv_b, m_wq, m_wk, m_wv, m_w_gate, m_b_gate, m_head_norm, m_skip, m_w_out,
               h_w_in, h_b_f, h_lb_logits, h_head_norm, h_w_out,
               ffn2_norm, ffn2_w_in, ffn2_w_out, ple_norm, ple_w_gate, ple_w_proj, final_norm)
    return (_trunk(x_prompt, p_prompt, *weights), _trunk(x_sample, p_sample, *weights))
```

```python
import functools

import jax
import jax.numpy as jnp
from jax import lax
from jax.experimental import pallas as pl
from jax.experimental.pallas import tpu as pltpu

F32 = jnp.float32
BF16 = jnp.bfloat16
EPS = 1e-6
LOG2_E = 1.4426950408889634

M_HEADS = 4
M_QKV_BLOCK = 4
M_CONV = 5
H_HEADS = 8
PLE_GROUP = 256
TOKEN_TILE = 512
SCAN_CHUNK = 256
SCAN_STEP_CHUNKS = 4
CONV_HALO = 16
GATE_LANES = 128
VMEM_LIMIT = 56 * 1024 * 1024


def _params(n_axes):
    return pltpu.CompilerParams(dimension_semantics=("arbitrary",) * n_axes,
                                vmem_limit_bytes=VMEM_LIMIT)


def _const_spec(shape):
    nd = len(shape)
    return pl.BlockSpec(shape, lambda *_: (0,) * nd, pipeline_mode=pl.Buffered(1))


def _rms(x, g):
    return x * lax.rsqrt(jnp.mean(x * x, axis=-1, keepdims=True) + EPS) * g


def _sigmoid(x):
    return 1.0 / (1.0 + jnp.exp(-x))


def _dot(a, b):
    return jnp.dot(a, b, preferred_element_type=F32)


def _dot_nt(a, b):
    return lax.dot_general(a, b, (((1,), (1,)), ((), ())), preferred_element_type=F32)


def _dot_tn(a, b):
    return lax.dot_general(a, b, (((0,), (0,)), ((), ())), preferred_element_type=F32)


def _mix_out(n_heads, ha_ref, hb_ref, hn_ref, add_ref, addw_ref, gate_ref, wo_ref, gate_fn):
    dh = ha_ref.shape[-1] // n_heads
    parts = []
    for h in range(n_heads):
        hs = slice(h * dh, (h + 1) * dh)
        hsum = ha_ref[:, hs].astype(F32) + hb_ref[:, hs].astype(F32)
        y = hsum * lax.rsqrt(jnp.mean(hsum * hsum, axis=-1, keepdims=True) + EPS) * hn_ref[:, hs]
        if add_ref is not None:
            y = y + addw_ref[:, hs] * add_ref[:, hs].astype(F32)
        parts.append((y * gate_fn(gate_ref[:, hs].astype(F32))).astype(BF16))
    return _dot(jnp.concatenate(parts, axis=-1), wo_ref[...])


def _ffn_body(mix, has_ple, has_final, *refs):
    it = iter(refs)
    x = next(it)[...]
    if mix == "mlstm":
        ha, hb, hn, xc, sk, z, wo = (next(it) for _ in range(7))
        x = x + _mix_out(M_HEADS, ha, hb, hn, xc, sk, z, wo, lambda t: t * _sigmoid(t))
    elif mix == "hgrn":
        ha, hb, hn, gg, wo = (next(it) for _ in range(5))
        x = x + _mix_out(H_HEADS, ha, hb, hn, None, None, gg, wo, _sigmoid)
    g_ref, win_ref, wout_ref = (next(it) for _ in range(3))
    if has_ple:
        p_ref, pg_ref, pwg_ref, pwp_ref = (next(it) for _ in range(4))
    if has_final:
        fg_ref = next(it)
    o_ref = next(it)
    d_ff = wout_ref.shape[0]
    xn = _rms(x, g_ref[...]).astype(BF16)
    a = _dot(xn, win_ref[:, :d_ff])
    u = _dot(xn, win_ref[:, d_ff:])
    act = (a * _sigmoid(a) * u).astype(BF16)
    x = x + 0.5 * _dot(act, wout_ref[...])
    if has_ple:
        xg = _rms(x, pg_ref[...]).astype(BF16)
        gate = _sigmoid(_dot(xg, pwg_ref[...]))
        x = x + gate * _dot(p_ref[...].astype(BF16), pwp_ref[...])
    if has_final:
        x = _rms(x, fg_ref[...])
    o_ref[...] = x


def _ffn(x, g, w_in, w_out, mix=None, ple=None, final_g=None):
    t, d = x.shape
    tm = TOKEN_TILE if mix is None else TOKEN_TILE // 2
    row = lambda w: pl.BlockSpec((tm, w), lambda i: (i, 0))
    args, specs = [x], [row(d)]
    kind = None
    if mix is not None:
        kind, operands, per_token = mix
        for item, tok in zip(operands, per_token):
            if item.ndim == 1:
                item = item.reshape(1, -1)
            args.append(item)
            specs.append(row(item.shape[1]) if tok else _const_spec(item.shape))
    args += [g.reshape(1, d), w_in, w_out]
    specs += [_const_spec((1, d)), _const_spec(w_in.shape), _const_spec(w_out.shape)]
    if ple is not None:
        p, pg, pwg, pwp = ple
        args += [p, pg.reshape(1, d), pwg, pwp]
        specs += [row(p.shape[1]), _const_spec((1, d)), _const_spec(pwg.shape), _const_spec(pwp.shape)]
    if final_g is not None:
        args.append(final_g.reshape(1, d))
        specs.append(_const_spec((1, d)))
    return pl.pallas_call(
        functools.partial(_ffn_body, kind, ple is not None, final_g is not None),
        grid=(t // tm,),
        in_specs=specs,
        out_specs=row(d),
        out_shape=jax.ShapeDtypeStruct((t, d), F32),
        compiler_params=_params(1),
        name="ffn" if mix is None else "mix_out_ffn",
    )(*args)


def _mpre_body(x_ref, g_ref, w_ref, xm_ref, z_ref):
    n = xm_ref.shape[-1]
    xn = _rms(x_ref[...], g_ref[...]).astype(BF16)
    xm_ref[...] = _dot(xn, w_ref[:, :n]).astype(BF16)
    z_ref[...] = _dot(xn, w_ref[:, n:]).astype(BF16)


def _mpre(x, g, w_in):
    t, d = x.shape
    n = w_in.shape[1] // 2
    tm = TOKEN_TILE
    row = lambda w: pl.BlockSpec((tm, w), lambda i: (i, 0))
    return pl.pallas_call(
        _mpre_body,
        grid=(t // tm,),
        in_specs=[row(d), _const_spec((1, d)), _const_spec(w_in.shape)],
        out_specs=[row(n), row(n)],
        out_shape=[jax.ShapeDtypeStruct((t, n), BF16)] * 2,
        compiler_params=_params(1),
        name="mlstm_pre",
    )(x, g.reshape(1, d), w_in)


def _log_sigmoid(x):
    return jnp.minimum(x, 0.0) - jnp.log1p(jnp.exp(-jnp.abs(x)))


def _mconv_body(k_scale, xm_ref, prev_ref, next_ref, cw_ref, cb_ref, wqk_ref, wv_ref, wg_ref, bg_ref,
                q_ref, k_ref, v_ref, xc_ref, g_ref, gt_ref, ext_ref):
    i = pl.program_id(1)
    last = pl.num_programs(1) - 1
    ts, c = xm_ref.shape[1], xm_ref.shape[2]
    hl = CONV_HALO
    pad = M_CONV // 2
    xm = xm_ref[0]
    ext_ref[0:hl, :] = jnp.where(i > 0, prev_ref[0].astype(F32), 0.0)
    ext_ref[hl:hl + ts, :] = xm.astype(F32)
    ext_ref[hl + ts:hl + ts + hl, :] = jnp.where(i < last, next_ref[0].astype(F32), 0.0)
    acc = jnp.broadcast_to(cb_ref[...], (ts, c))
    for j in range(M_CONV):
        acc = acc + cw_ref[j:j + 1, :] * ext_ref[hl - pad + j:hl - pad + j + ts, :]
    xc = (acc * _sigmoid(acc)).astype(BF16)
    xc_ref[0] = xc
    gw = PLE_GROUP
    gates = jnp.broadcast_to(bg_ref[...], (ts, GATE_LANES))
    for j in range(c // gw):
        cs = slice(j * gw, (j + 1) * gw)
        qk = _dot(xc[:, cs], wqk_ref[j])
        qj = qk[:, :gw].astype(BF16)
        kj = qk[:, gw:]
        kjb = kj.astype(BF16)
        vj = _dot(xm[:, cs], wv_ref[j]).astype(BF16)
        q_ref[0, :, cs] = qj
        k_ref[0, :, cs] = (kj * k_scale).astype(BF16)
        v_ref[0, :, cs] = vj
        gates = gates + _dot(qj, wg_ref[0, cs, :]) + _dot(kjb, wg_ref[1, cs, :]) + _dot(vj, wg_ref[2, cs, :])
    col = lax.broadcasted_iota(jnp.int32, gates.shape, 1)
    is_forget = (col % (2 * M_HEADS)) >= M_HEADS
    gates = jnp.where(is_forget, _log_sigmoid(gates), gates)
    g_ref[0] = gates
    gates_t = gates.T
    for ci in range(ts // SCAN_CHUNK):
        gt_ref[0, ci] = gates_t[:4 * M_HEADS, ci * SCAN_CHUNK:(ci + 1) * SCAN_CHUNK]


def _mconv(xm, conv_w, conv_b, wqk, wv, wg, bg, k_scale):
    b, s, c = xm.shape
    ts = TOKEN_TILE
    hb = ts // CONV_HALO
    nh_blocks = s // CONV_HALO
    tile = lambda w, dt=None: pl.BlockSpec((1, ts, w), lambda bi, i: (bi, i, 0))
    return pl.pallas_call(
        functools.partial(_mconv_body, k_scale),
        grid=(b, s // ts),
        in_specs=[
            tile(c),
            pl.BlockSpec((1, CONV_HALO, c), lambda bi, i: (bi, jnp.maximum(i * hb - 1, 0), 0)),
            pl.BlockSpec((1, CONV_HALO, c), lambda bi, i: (bi, jnp.minimum((i + 1) * hb, nh_blocks - 1), 0)),
            _const_spec(conv_w.shape), _const_spec((1, c)),
            _const_spec(wqk.shape), _const_spec(wv.shape), _const_spec(wg.shape), _const_spec(bg.shape),
        ],
        out_specs=[tile(c), tile(c), tile(c), tile(c), tile(GATE_LANES),
                   pl.BlockSpec((1, ts // SCAN_CHUNK, 4 * M_HEADS, SCAN_CHUNK), lambda bi, i: (bi, i, 0, 0))],
        out_shape=[jax.ShapeDtypeStruct((b, s, c), BF16)] * 4
        + [jax.ShapeDtypeStruct((b, s, GATE_LANES), F32),
           jax.ShapeDtypeStruct((b, s // SCAN_CHUNK, 4 * M_HEADS, SCAN_CHUNK), F32)],
        scratch_shapes=[pltpu.VMEM((ts + 2 * CONV_HALO, c), F32)],
        compiler_params=_params(2),
        name="mlstm_conv_qkv",
    )(xm, xm, xm, conv_w, conv_b.reshape(1, c), wqk, wv, wg, bg)


def _running(x, axis, rev, op, fill):
    n = x.shape[axis]
    idx = lax.broadcasted_iota(jnp.int32, x.shape, axis)
    s = 1
    while s < n:
        if rev:
            x = op(x, jnp.where(idx < n - s, pltpu.roll(x, n - s, axis), fill))
        else:
            x = op(x, jnp.where(idx >= s, pltpu.roll(x, s, axis), fill))
        s *= 2
    return x


def _zero_rows_at_first_chunk(refs, rows_per_trip):
    rows = refs[0].shape[0]

    def zero(i, carry):
        r0 = pl.multiple_of(i * rows_per_trip, rows_per_trip)
        for ref in refs:
            ref[pl.ds(r0, rows_per_trip), :] = jnp.zeros((rows_per_trip, ref.shape[1]), F32)
        return carry

    lax.fori_loop(0, jnp.where(pl.program_id(1) == 0, rows // rows_per_trip, 0), zero, 0)


def _mlstm_body(rev, q_ref, k_ref, v_ref, g_ref, gt_ref, o_ref, *state_refs):
    nh = M_HEADS
    c_refs, n_refs, m_refs = state_refs[:nh], state_refs[nh:2 * nh], state_refs[2 * nh:]
    ln = SCAN_CHUNK
    n_chunks = q_ref.shape[1] // ln
    dh = q_ref.shape[2] // nh
    _zero_rows_at_first_chunk(c_refs, 64)
    _zero_rows_at_first_chunk(n_refs + m_refs, 8)

    tt = lax.broadcasted_iota(jnp.int32, (ln, ln), 0)
    ss = lax.broadcasted_iota(jnp.int32, (ln, ln), 1)
    valid = (ss >= tt) if rev else (ss <= tt)
    end = 0 if rev else ln - 1
    goff = 2 * nh if rev else 0

    def chunk(i, carry):
        ck = (n_chunks - 1 - i) if rev else i
        rows = pl.ds(pl.multiple_of(ck * ln, ln), ln)
        g = g_ref[0, rows, :] * LOG2_E
        gt = gt_ref[0, ck] * LOG2_E
        bcol = _running(g, 0, rev, jnp.add, 0.0)
        brow = _running(gt, 1, rev, jnp.add, 0.0)
        gdcol = g - pltpu.roll(bcol, g.shape[1] - nh, 1)
        gdrow = gt - pltpu.roll(brow, gt.shape[0] - nh, 0)
        mcol = _running(gdcol, 0, rev, jnp.maximum, -jnp.inf)
        for h in range(nh):
            hs = slice(h * dh, (h + 1) * dh)
            ci, cf = goff + h, goff + nh + h
            gd_row, gd_col = gdrow[ci:ci + 1, :], gdcol[:, ci:ci + 1]
            b_col = bcol[:, cf:cf + 1]
            b_all = b_col[end:end + 1, :]
            m_prev = m_refs[h][0:1, 0:1]
            big_m = jnp.maximum(mcol[:, ci:ci + 1], m_prev)
            m_last = big_m[end:end + 1, :]
            q, k, v = q_ref[0, rows, hs], k_ref[0, rows, hs], v_ref[0, rows, hs]

            s_mat = _dot_nt(q, k) * jnp.exp2(jnp.where(valid, gd_row - big_m, -jnp.inf))
            w_inter = jnp.exp2(m_prev - big_m)
            cst = c_refs[h][...]
            nst = n_refs[h][0:1, :]
            num = _dot(s_mat.astype(BF16), v) + w_inter * _dot(q, cst.astype(BF16))
            den = jnp.sum(s_mat, axis=-1, keepdims=True) + \
                w_inter * jnp.sum(q.astype(F32) * nst, axis=-1, keepdims=True)
            hh = num * (1.0 / jnp.maximum(jnp.abs(den), jnp.exp2(-(b_col + big_m))))
            o_ref[0, rows, hs] = hh.astype(o_ref.dtype)

            wk_col = jnp.exp2(gd_col - m_last)
            w_old = jnp.exp2(m_prev - m_last)
            kw = k.astype(F32) * wk_col
            c_refs[h][...] = w_old * cst + _dot_tn(kw.astype(BF16), v)
            n_refs[h][...] = jnp.broadcast_to(w_old * nst + jnp.sum(kw, axis=0, keepdims=True), n_refs[h].shape)
            m_refs[h][...] = jnp.broadcast_to(b_all + m_last, m_refs[h].shape)
        return carry

    lax.fori_loop(0, n_chunks, chunk, 0)


def _mlstm_scan(rev, q, k, v, g, gt):
    b, s, c = q.shape
    ln = SCAN_CHUNK * SCAN_STEP_CHUNKS
    nc = s // ln
    dh = c // M_HEADS
    cidx = (lambda i: nc - 1 - i) if rev else (lambda i: i)
    tile = lambda w: pl.BlockSpec((1, ln, w), lambda bi, i: (bi, cidx(i), 0))
    return pl.pallas_call(
        functools.partial(_mlstm_body, rev),
        grid=(b, nc),
        in_specs=[tile(c), tile(c), tile(c), tile(GATE_LANES),
                  pl.BlockSpec((1, SCAN_STEP_CHUNKS, 4 * M_HEADS, SCAN_CHUNK), lambda bi, i: (bi, cidx(i), 0, 0))],
        out_specs=tile(c),
        out_shape=jax.ShapeDtypeStruct((b, s, c), BF16),
        scratch_shapes=[pltpu.VMEM((dh, dh), F32)] * M_HEADS + [pltpu.VMEM((8, dh), F32)] * M_HEADS
        + [pltpu.VMEM((8, 128), F32)] * M_HEADS,
        compiler_params=_params(2),
        name="mlstm_scan_bwd" if rev else "mlstm_scan_fwd",
    )(q, k, v, g, gt)


def _blockdiag_tiles(w):
    nb, bc, bd = w.shape
    per = PLE_GROUP // bc
    w = w.reshape(nb // per, per, bc, bd)
    eye = jnp.eye(per, dtype=w.dtype)
    dense = jnp.einsum('jncd,nm->jncmd', w, eye)
    return dense.reshape(nb // per, per * bc, per * bd)


def _mlstm_mixer(x, mix_g, w_in, conv_w, conv_b, wq, wk, wv, w_gate, b_gate, head_norm, skip, w_out):
    b, s, d = x.shape
    c = w_out.shape[0]
    xm, z = _mpre(x.reshape(b * s, d), mix_g, w_in.astype(BF16))
    xm = xm.reshape(b, s, c)
    wqk = jnp.concatenate([_blockdiag_tiles(wq), _blockdiag_tiles(wk)], axis=-1).astype(BF16)
    wvt = _blockdiag_tiles(wv).astype(BF16)
    ng = w_gate.shape[1]
    wg = jnp.pad(w_gate.reshape(3, c, ng), ((0, 0), (0, 0), (0, GATE_LANES - ng))).astype(BF16)
    bg = jnp.pad(b_gate.reshape(1, ng), ((0, 0), (0, GATE_LANES - ng)))
    k_scale = float(c // M_HEADS) ** -0.5
    q, k, v, xc, g, gt = _mconv(xm, conv_w, conv_b, wqk, wvt, wg, bg, k_scale)
    flat = lambda t: t.reshape(b * s, c)
    h_f = _mlstm_scan(False, q, k, v, g, gt)
    h_b = _mlstm_scan(True, q, k, v, g, gt)
    return ("mlstm", (flat(h_f), flat(h_b), head_norm, flat(xc), skip, z, w_out.astype(BF16)),
            (True, True, False, True, False, True, False))


def _hpre_body(x_ref, g_ref, w_ref, bf_ref, lb_ref, q_ref, kf_ref, kb_ref, v_ref, lff_ref, lfb_ref, gg_ref):
    d = x_ref.shape[-1]
    xn = _rms(x_ref[...], g_ref[...]).astype(BF16)
    proj = lambda col: _dot(xn, w_ref[:, col * d:(col + 1) * d])
    q_ref[...] = proj(0).astype(BF16)
    v_ref[...] = proj(3).astype(BF16)
    gg_ref[...] = proj(4).astype(BF16)
    for di, (k_ref, lf_ref) in enumerate(((kf_ref, lff_ref), (kb_ref, lfb_ref))):
        a = proj(1 + di) + bf_ref[di:di + 1, :]
        lb = lb_ref[di:di + 1, :]
        e = jnp.exp(-jnp.abs(a))
        inv = 1.0 / (1.0 + e)
        pos = a >= 0.0
        sig = jnp.where(pos, inv, e * inv)
        nsig = jnp.where(pos, e * inv, inv)
        lf_ref[...] = jnp.log(lb + (1.0 - lb) * sig) * LOG2_E
        k_ref[...] = ((1.0 - lb) * nsig).astype(BF16)


def _hpre(x, g, w_in, b_f, lb):
    t, d = x.shape
    tm = TOKEN_TILE
    row = pl.BlockSpec((tm, d), lambda i: (i, 0))
    shape = lambda dt: jax.ShapeDtypeStruct((t, d), dt)
    return pl.pallas_call(
        _hpre_body,
        grid=(t // tm,),
        in_specs=[row, _const_spec((1, d)), _const_spec(w_in.shape), _const_spec((2, d)), _const_spec((2, d))],
        out_specs=[row] * 7,
        out_shape=[shape(BF16)] * 4 + [shape(F32)] * 2 + [shape(BF16)],
        compiler_params=_params(1),
        name="hgrn_pre",
    )(x, g.reshape(1, d), w_in, b_f, lb)


def _hgrn_body(rev, q_ref, k_ref, v_ref, lf_ref, o_ref, *st_refs):
    nh = len(st_refs)
    ln = SCAN_CHUNK
    n_chunks = q_ref.shape[1] // ln
    dk = q_ref.shape[2] // nh
    hb = ln // 2
    sub = 8
    ng = ln // sub
    qside = 0 if rev else 1
    _zero_rows_at_first_chunk(st_refs, 32)

    r8 = lax.broadcasted_iota(jnp.int32, (ng, sub, dk), 1)
    tt = lax.broadcasted_iota(jnp.int32, (hb, hb), 0)
    ss = lax.broadcasted_iota(jnp.int32, (hb, hb), 1)
    lvl = (lax.bitcast_convert_type((tt ^ ss).astype(F32), jnp.int32) >> 23) - 127
    lvl = jnp.where((ss > tt) if rev else (ss < tt), lvl, jnp.where(tt == ss, -1, -2))
    halves = (slice(0, hb), slice(hb, ln))
    bcast = lambda t, i: jnp.broadcast_to(t[:, i:i + 1, :], t.shape)

    def head(h, rows):
        hs = slice(h * dk, (h + 1) * dk)
        lf = lf_ref[0, rows, hs]
        q, k, v = q_ref[0, rows, hs], k_ref[0, rows, hs], v_ref[0, rows, hs]
        qf, kf = q.astype(F32), k.astype(F32)
        z_up, z_lo = (kf, qf) if rev else (qf, kf)
        a = [jnp.where(lvl == -1, _dot_nt(q[hv], k[hv]), 0.0) for hv in halves]

        lf3 = lf.reshape(ng, sub, dk)
        z3_up, z3_lo = z_up.reshape(ng, sub, dk), z_lo.reshape(ng, sub, dk)
        cc = lf3
        for s in (1, 2, 4):
            if rev:
                cc = cc + jnp.where(r8 < sub - s, pltpu.roll(cc, sub - s, 1), 0.0)
            else:
                cc = cc + jnp.where(r8 >= s, pltpu.roll(cc, s, 1), 0.0)
        for j, m in enumerate((1, 2, 4)):
            upper = (r8 & m) != 0
            ref = m if rev else m - 1
            if m == 1:
                e = jnp.where(upper != rev, lf3, 0.0)
            elif m == 2:
                e = -jnp.abs(cc - jnp.where(r8 < 4, bcast(cc, ref), bcast(cc, ref + 4)))
            else:
                e = -jnp.abs(cc - bcast(cc, ref))
            w = (jnp.where(upper, z3_up, z3_lo) * jnp.exp2(e)).reshape(ln, dk).astype(BF16)
            for i, hv in enumerate(halves):
                a[i] = jnp.where(lvl == j, _dot_nt(w[hv], w[hv]), a[i])
        if rev:
            y = cc
            tot = bcast(cc, 0)
            x = tot - cc
        else:
            x = cc
            tot = bcast(cc, sub - 1)
            y = tot - cc
        x, y, tot = x.reshape(ln, dk), y.reshape(ln, dk), tot.reshape(ln, dk)

        m, j = sub, 3
        while m < ln:
            nb = ln // (2 * m)
            split = lambda t: t.reshape(nb, 2, m, t.shape[-1])
            join = lambda lo, up: jnp.stack([lo, up], axis=1).reshape(ln, dk)
            x4, y4, t4 = split(x), split(y), split(tot)
            w_lo = split(z_lo)[:, 0] * jnp.exp2(y4[:, 0])
            w_up = split(z_up)[:, 1] * jnp.exp2(x4[:, 1])
            if m < hb:
                w = join(w_lo, w_up).astype(BF16)
                wq = (w_lo if rev else w_up).astype(BF16)
                nbh = nb // 2
                mask = lvl.reshape(nbh, 2, m, hb)[:, qside] == j
                for i, hv in enumerate(halves):
                    p = _dot_nt(wq[i * nbh:(i + 1) * nbh].reshape(nbh * m, dk), w[hv])
                    a4 = a[i].reshape(nbh, 2, m, hb)
                    aq = jnp.where(mask, p.reshape(nbh, m, hb), a4[:, qside])
                    parts = [aq, a4[:, 1]] if rev else [a4[:, 0], aq]
                    a[i] = jnp.stack(parts, axis=1).reshape(hb, hb)
            else:
                wl, wu = w_lo.reshape(hb, dk).astype(BF16), w_up.reshape(hb, dk).astype(BF16)
                cross = _dot_nt(wl, wu) if rev else _dot_nt(wu, wl)
            x = join(x4[:, 0], x4[:, 1] + t4[:, 0])
            y = join(y4[:, 0] + t4[:, 1], y4[:, 1])
            tsum = t4[:, 0] + t4[:, 1]
            tot = join(tsum, tsum)
            m, j = 2 * m, j + 1
        e_q, e_k = (y, x) if rev else (x, y)
        st = st_refs[h][...]
        inter = _dot_nt((qf * jnp.exp2(e_q)).astype(BF16), st.astype(BF16))
        a0, a1, cr = a[0].astype(BF16), a[1].astype(BF16), cross.astype(BF16)
        if rev:
            o0 = _dot(jnp.concatenate([a0, cr], axis=1), v)
            o1 = _dot(a1, v[halves[1]])
        else:
            o0 = _dot(a0, v[halves[0]])
            o1 = _dot(jnp.concatenate([cr, a1], axis=1), v)
        out = jnp.concatenate([o0 + inter[halves[0]], o1 + inter[halves[1]]], axis=0)
        o_ref[0, rows, hs] = out.astype(o_ref.dtype)
        kw = (kf * jnp.exp2(e_k)).astype(BF16)
        st_refs[h][...] = jnp.exp2(tot[0:1, :]) * st + _dot_tn(v, kw)

    def chunk(i, carry):
        ck = (n_chunks - 1 - i) if rev else i
        rows = pl.ds(pl.multiple_of(ck * ln, ln), ln)
        for h in range(nh):
            head(h, rows)
        return carry

    lax.fori_loop(0, n_chunks, chunk, 0)


def _hgrn_scan(rev, q, k, v, lf):
    b, s, d = q.shape
    nh = H_HEADS
    dk = d // nh
    ln = SCAN_CHUNK * SCAN_STEP_CHUNKS
    nc = s // ln
    cidx = (lambda i: nc - 1 - i) if rev else (lambda i: i)
    tile = pl.BlockSpec((1, ln, d), lambda bi, i: (bi, cidx(i), 0))
    return pl.pallas_call(
        functools.partial(_hgrn_body, rev),
        grid=(b, nc),
        in_specs=[tile] * 4,
        out_specs=tile,
        out_shape=jax.ShapeDtypeStruct((b, s, d), BF16),
        scratch_shapes=[pltpu.VMEM((dk, dk), F32)] * nh,
        compiler_params=_params(2),
        name="hgrn_scan_bwd" if rev else "hgrn_scan_fwd",
    )(q, k, v, lf)


def _lower_bounds(logits):
    pr = jax.nn.softmax(logits.astype(F32), axis=0)
    return jnp.cumsum(pr, axis=0) - pr[0]


def _hgrn_mixer(x, mix_g, w_in, b_f, lb, head_norm, w_out):
    b, s, d = x.shape
    q, k_f, k_b, v, lf_f, lf_b, gg = _hpre(x.reshape(b * s, d), mix_g, w_in.astype(BF16), b_f, lb)
    seq = lambda t: t.reshape(b, s, d)
    o_f = _hgrn_scan(False, seq(q), seq(k_f), seq(v), seq(lf_f))
    o_b = _hgrn_scan(True, seq(q), seq(k_b), seq(v), seq(lf_b))
    flat = lambda t: t.reshape(b * s, d)
    return "hgrn", (flat(o_f), flat(o_b), head_norm, gg, w_out.astype(BF16)), (True, True, False, True, False)


def _trunk(x, p, ffn1_norm, ffn1_w_in, ffn1_w_out, mix_norm,
           m_w_in, m_conv_w, m_conv_b, m_wq, m_wk, m_wv, m_w_gate, m_b_gate, m_head_norm, m_skip, m_w_out,
           h_w_in, h_b_f, h_lb_logits, h_head_norm, h_w_out,
           ffn2_norm, ffn2_w_in, ffn2_w_out, ple_norm, ple_w_gate, ple_w_proj, final_norm):
    b, s, d = x.shape
    depth = ffn1_norm.shape[0]
    n_mixers = 2
    lb_fwd = _lower_bounds(h_lb_logits[0])
    lb_bwd = _lower_bounds(h_lb_logits[1])
    x = x.reshape(b * s, d)
    for i in range(depth):
        x = _ffn(x, ffn1_norm[i], ffn1_w_in[i].astype(BF16), ffn1_w_out[i].astype(BF16))
        j = i // n_mixers
        xs = x.reshape(b, s, d)
        if i % n_mixers == 0:
            mix = _mlstm_mixer(xs, mix_norm[i], m_w_in[j], m_conv_w[j], m_conv_b[j], m_wq[j], m_wk[j], m_wv[j],
                               m_w_gate[j], m_b_gate[j], m_head_norm[j], m_skip[j], m_w_out[j])
        else:
            lb = jnp.stack([lb_fwd[i], lb_bwd[i]])
            mix = _hgrn_mixer(xs, mix_norm[i], h_w_in[j], h_b_f[j], lb, h_head_norm[j], h_w_out[j])
        ple = (p[i].reshape(b * s, -1), ple_norm[i], ple_w_gate[i].astype(BF16), ple_w_proj[i].astype(BF16))
        x = _ffn(x, ffn2_norm[i], ffn2_w_in[i].astype(BF16), ffn2_w_out[i].astype(BF16), mix=mix, ple=ple,
                 final_g=final_norm if i == depth - 1 else None)
    return x.reshape(b, s, d)


def kernel(x_prompt, x_sample, p_prompt, p_sample, ffn1_norm, ffn1_w_in, ffn1_w_out, mix_norm, m_w_in, m_conv_w, m_conv_b, m_wq, m_wk, m_wv, m_w_gate, m_b_gate, m_head_norm, m_skip, m_w_out, h_w_in, h_b_f, h_lb_logits, h_head_norm, h_w_out, ffn2_norm, ffn2_w_in, ffn2_w_out, ple_norm, ple_w_gate, ple_w_proj, final_norm):
    weights = (ffn1_norm, ffn1_w_in, ffn1_w_out, mix_norm,
               m_w_in, m_conv_w, m_conv_b, m_wq, m_wk, m_wv, m_w_gate, m_b_gate, m_head_norm, m_skip, m_w_out,
               h_w_in, h_b_f, h_lb_logits, h_head_norm, h_w_out,
               ffn2_norm, ffn2_w_in, ffn2_w_out, ple_norm, ple_w_gate, ple_w_proj, final_norm)
    return (_trunk(x_prompt, p_prompt, *weights), _trunk(x_sample, p_sample, *weights))
```

```python
import functools

import jax
import jax.numpy as jnp
from jax import lax
from jax.experimental import pallas as pl
from jax.experimental.pallas import tpu as pltpu

F32 = jnp.float32
BF16 = jnp.bfloat16
EPS = 1e-6
LOG2_E = 1.4426950408889634

M_HEADS = 4
M_QKV_BLOCK = 4
M_CONV = 5
H_HEADS = 8
PLE_GROUP = 256
TOKEN_TILE = 512
SCAN_CHUNK = 256
SCAN_STEP_CHUNKS = 4
CONV_HALO = 8
GATE_LANES = 128
VMEM_LIMIT = 56 * 1024 * 1024


def _params(n_axes):
    return pltpu.CompilerParams(dimension_semantics=("arbitrary",) * n_axes,
                                vmem_limit_bytes=VMEM_LIMIT)


def _const_spec(shape):
    nd = len(shape)
    return pl.BlockSpec(shape, lambda *_: (0,) * nd, pipeline_mode=pl.Buffered(1))


def _rms(x, g):
    return x * lax.rsqrt(jnp.mean(x * x, axis=-1, keepdims=True) + EPS) * g


def _sigmoid(x):
    return 1.0 / (1.0 + jnp.exp(-x))


def _dot(a, b):
    return jnp.dot(a, b, preferred_element_type=F32)


def _dot_nt(a, b):
    return lax.dot_general(a, b, (((1,), (1,)), ((), ())), preferred_element_type=F32)


def _dot_tn(a, b):
    return lax.dot_general(a, b, (((0,), (0,)), ((), ())), preferred_element_type=F32)


def _mix_out(n_heads, ha_ref, hb_ref, hn_ref, add_ref, addw_ref, gate_ref, wo_ref, gate_fn):
    dh = ha_ref.shape[-1] // n_heads
    parts = []
    for h in range(n_heads):
        hs = slice(h * dh, (h + 1) * dh)
        hsum = ha_ref[:, hs].astype(F32) + hb_ref[:, hs].astype(F32)
        y = hsum * lax.rsqrt(jnp.mean(hsum * hsum, axis=-1, keepdims=True) + EPS) * hn_ref[:, hs]
        if add_ref is not None:
            y = y + addw_ref[:, hs] * add_ref[:, hs].astype(F32)
        parts.append((y * gate_fn(gate_ref[:, hs].astype(F32))).astype(BF16))
    return _dot(jnp.concatenate(parts, axis=-1), wo_ref[...])


def _ffn_body(mix, has_ple, has_final, *refs):
    it = iter(refs)
    x = next(it)[...]
    if mix == "mlstm":
        ha, hb, hn, xc, sk, z, wo = (next(it) for _ in range(7))
        x = x + _mix_out(M_HEADS, ha, hb, hn, xc, sk, z, wo, lambda t: t * _sigmoid(t))
    elif mix == "hgrn":
        ha, hb, hn, gg, wo = (next(it) for _ in range(5))
        x = x + _mix_out(H_HEADS, ha, hb, hn, None, None, gg, wo, _sigmoid)
    g_ref, win_ref, wout_ref = (next(it) for _ in range(3))
    if has_ple:
        p_ref, pg_ref, pwg_ref, pwp_ref = (next(it) for _ in range(4))
    if has_final:
        fg_ref = next(it)
    o_ref = next(it)
    d_ff = wout_ref.shape[0]
    xn = _rms(x, g_ref[...]).astype(BF16)
    a = _dot(xn, win_ref[:, :d_ff])
    u = _dot(xn, win_ref[:, d_ff:])
    act = (a * _sigmoid(a) * u).astype(BF16)
    x = x + 0.5 * _dot(act, wout_ref[...])
    if has_ple:
        xg = _rms(x, pg_ref[...]).astype(BF16)
        gate = _sigmoid(_dot(xg, pwg_ref[...]))
        x = x + gate * _dot(p_ref[...].astype(BF16), pwp_ref[...])
    if has_final:
        x = _rms(x, fg_ref[...])
    o_ref[...] = x


def _ffn(x, g, w_in, w_out, mix=None, ple=None, final_g=None):
    t, d = x.shape
    tm = TOKEN_TILE if mix is None else TOKEN_TILE // 2
    row = lambda w: pl.BlockSpec((tm, w), lambda i: (i, 0))
    args, specs = [x], [row(d)]
    kind = None
    if mix is not None:
        kind, operands, per_token = mix
        for item, tok in zip(operands, per_token):
            if item.ndim == 1:
                item = item.reshape(1, -1)
            args.append(item)
            specs.append(row(item.shape[1]) if tok else _const_spec(item.shape))
    args += [g.reshape(1, d), w_in, w_out]
    specs += [_const_spec((1, d)), _const_spec(w_in.shape), _const_spec(w_out.shape)]
    if ple is not None:
        p, layer, pg, pwg, pwp = ple
        args += [p, pg.reshape(1, d), pwg, pwp]
        specs += [pl.BlockSpec((None, tm, p.shape[2]), lambda i: (layer, i, 0)),
                  _const_spec((1, d)), _const_spec(pwg.shape), _const_spec(pwp.shape)]
    if final_g is not None:
        args.append(final_g.reshape(1, d))
        specs.append(_const_spec((1, d)))
    return pl.pallas_call(
        functools.partial(_ffn_body, kind, ple is not None, final_g is not None),
        grid=(t // tm,),
        in_specs=specs,
        out_specs=row(d),
        out_shape=jax.ShapeDtypeStruct((t, d), F32),
        compiler_params=_params(1),
        name="ffn" if mix is None else "mix_out_ffn",
    )(*args)


def _log_sigmoid(x):
    return jnp.minimum(x, 0.0) - jnp.log1p(jnp.exp(-jnp.abs(x)))


def _min_body(k_scale, x_ref, xprev_ref, xnext_ref, ng_ref, win_ref, cw_ref, cb_ref, wqk_ref, wv_ref, wg_ref,
              bg_ref, q_ref, k_ref, v_ref, xc_ref, z_ref, g_ref, gt_ref, ext_ref):
    i = pl.program_id(1)
    last = pl.num_programs(1) - 1
    ts = x_ref.shape[1]
    c = q_ref.shape[2]
    hl = CONV_HALO
    pad = M_CONV // 2
    ng = ng_ref[...]
    xn = _rms(x_ref[0], ng)
    xn_ext = jnp.concatenate([_rms(xprev_ref[0], ng), xn, _rms(xnext_ref[0], ng)], axis=0).astype(BF16)
    xn = xn.astype(BF16)
    z_ref[0] = _dot(xn, win_ref[:, c:]).astype(BF16)
    xm_ext = _dot(xn_ext, win_ref[:, :c])
    r = lax.broadcasted_iota(jnp.int32, (ts + 2 * hl, 1), 0)
    inside = ((r >= hl) | (i > 0)) & ((r < hl + ts) | (i < last))
    ext_ref[...] = jnp.where(inside, xm_ext, 0.0)
    xm = ext_ref[hl:hl + ts, :].astype(BF16)
    acc = jnp.broadcast_to(cb_ref[...], (ts, c))
    for j in range(M_CONV):
        acc = acc + cw_ref[j:j + 1, :] * ext_ref[hl - pad + j:hl - pad + j + ts, :]
    xc = (acc * _sigmoid(acc)).astype(BF16)
    xc_ref[0] = xc
    gw = PLE_GROUP
    gates = jnp.broadcast_to(bg_ref[...], (ts, GATE_LANES))
    for j in range(c // gw):
        cs = slice(j * gw, (j + 1) * gw)
        qk = _dot(xc[:, cs], wqk_ref[j])
        qj = qk[:, :gw].astype(BF16)
        kj = qk[:, gw:]
        kjb = kj.astype(BF16)
        vj = _dot(xm[:, cs], wv_ref[j]).astype(BF16)
        q_ref[0, :, cs] = qj
        k_ref[0, :, cs] = (kj * k_scale).astype(BF16)
        v_ref[0, :, cs] = vj
        gates = gates + _dot(qj, wg_ref[0, cs, :]) + _dot(kjb, wg_ref[1, cs, :]) + _dot(vj, wg_ref[2, cs, :])
    col = lax.broadcasted_iota(jnp.int32, gates.shape, 1)
    is_forget = (col % (2 * M_HEADS)) >= M_HEADS
    gates = jnp.where(is_forget, _log_sigmoid(gates), gates)
    g_ref[0] = gates
    gates_t = gates.T
    for ci in range(ts // SCAN_CHUNK):
        gt_ref[0, ci] = gates_t[:4 * M_HEADS, ci * SCAN_CHUNK:(ci + 1) * SCAN_CHUNK]


def _mlstm_in(x, norm_g, w_in, conv_w, conv_b, wqk, wv, wg, bg, k_scale):
    b, s, d = x.shape
    c = w_in.shape[1] // 2
    ts = TOKEN_TILE
    hb = ts // CONV_HALO
    nh_blocks = s // CONV_HALO
    tile = lambda w: pl.BlockSpec((1, ts, w), lambda bi, i: (bi, i, 0))
    return pl.pallas_call(
        functools.partial(_min_body, k_scale),
        grid=(b, s // ts),
        in_specs=[
            tile(d),
            pl.BlockSpec((1, CONV_HALO, d), lambda bi, i: (bi, jnp.maximum(i * hb - 1, 0), 0)),
            pl.BlockSpec((1, CONV_HALO, d), lambda bi, i: (bi, jnp.minimum((i + 1) * hb, nh_blocks - 1), 0)),
            _const_spec((1, d)), _const_spec(w_in.shape), _const_spec(conv_w.shape), _const_spec((1, c)),
            _const_spec(wqk.shape), _const_spec(wv.shape), _const_spec(wg.shape), _const_spec(bg.shape),
        ],
        out_specs=[tile(c)] * 5 + [tile(GATE_LANES),
                                   pl.BlockSpec((1, ts // SCAN_CHUNK, 4 * M_HEADS, SCAN_CHUNK),
                                                lambda bi, i: (bi, i, 0, 0))],
        out_shape=[jax.ShapeDtypeStruct((b, s, c), BF16)] * 5
        + [jax.ShapeDtypeStruct((b, s, GATE_LANES), F32),
           jax.ShapeDtypeStruct((b, s // SCAN_CHUNK, 4 * M_HEADS, SCAN_CHUNK), F32)],
        scratch_shapes=[pltpu.VMEM((ts + 2 * CONV_HALO, c), F32)],
        compiler_params=_params(2),
        name="mlstm_in",
    )(x, x, x, norm_g.reshape(1, d), w_in, conv_w, conv_b.reshape(1, c), wqk, wv, wg, bg)


def _running(x, axis, rev, op, fill):
    n = x.shape[axis]
    idx = lax.broadcasted_iota(jnp.int32, x.shape, axis)
    s = 1
    while s < n:
        if rev:
            x = op(x, jnp.where(idx < n - s, pltpu.roll(x, n - s, axis), fill))
        else:
            x = op(x, jnp.where(idx >= s, pltpu.roll(x, s, axis), fill))
        s *= 2
    return x


def _zero_rows_at_first_chunk(refs, rows_per_trip):
    rows = refs[0].shape[0]

    def zero(i, carry):
        r0 = pl.multiple_of(i * rows_per_trip, rows_per_trip)
        for ref in refs:
            ref[pl.ds(r0, rows_per_trip), :] = jnp.zeros((rows_per_trip, ref.shape[1]), F32)
        return carry

    lax.fori_loop(0, jnp.where(pl.program_id(1) == 0, rows // rows_per_trip, 0), zero, 0)


def _mlstm_body(rev, q_ref, k_ref, v_ref, g_ref, gt_ref, o_ref, *state_refs):
    nh = M_HEADS
    c_refs, n_refs, m_refs = state_refs[:nh], state_refs[nh:2 * nh], state_refs[2 * nh:]
    ln = SCAN_CHUNK
    n_chunks = q_ref.shape[1] // ln
    dh = q_ref.shape[2] // nh
    _zero_rows_at_first_chunk(c_refs, 64)
    _zero_rows_at_first_chunk(n_refs + m_refs, 8)

    tt = lax.broadcasted_iota(jnp.int32, (ln, ln), 0)
    ss = lax.broadcasted_iota(jnp.int32, (ln, ln), 1)
    valid = (ss >= tt) if rev else (ss <= tt)
    end = 0 if rev else ln - 1
    goff = 2 * nh if rev else 0

    def chunk(i, carry):
        ck = (n_chunks - 1 - i) if rev else i
        rows = pl.ds(pl.multiple_of(ck * ln, ln), ln)
        g = g_ref[0, rows, :] * LOG2_E
        gt = gt_ref[0, ck] * LOG2_E
        bcol = _running(g, 0, rev, jnp.add, 0.0)
        brow = _running(gt, 1, rev, jnp.add, 0.0)
        gdcol = g - pltpu.roll(bcol, g.shape[1] - nh, 1)
        gdrow = gt - pltpu.roll(brow, gt.shape[0] - nh, 0)
        mcol = _running(gdcol, 0, rev, jnp.maximum, -jnp.inf)
        for h in range(nh):
            hs = slice(h * dh, (h + 1) * dh)
            ci, cf = goff + h, goff + nh + h
            gd_row, gd_col = gdrow[ci:ci + 1, :], gdcol[:, ci:ci + 1]
            b_col = bcol[:, cf:cf + 1]
            b_all = b_col[end:end + 1, :]
            m_prev = m_refs[h][0:1, 0:1]
            big_m = jnp.maximum(mcol[:, ci:ci + 1], m_prev)
            m_last = big_m[end:end + 1, :]
            q, k, v = q_ref[0, rows, hs], k_ref[0, rows, hs], v_ref[0, rows, hs]

            s_mat = _dot_nt(q, k) * jnp.exp2(jnp.where(valid, gd_row - big_m, -jnp.inf))
            w_inter = jnp.exp2(m_prev - big_m)
            cst = c_refs[h][...]
            nst = n_refs[h][0:1, :]
            num = _dot(s_mat.astype(BF16), v) + w_inter * _dot(q, cst.astype(BF16))
            den = jnp.sum(s_mat, axis=-1, keepdims=True) + \
                w_inter * jnp.sum(q.astype(F32) * nst, axis=-1, keepdims=True)
            hh = num * (1.0 / jnp.maximum(jnp.abs(den), jnp.exp2(-(b_col + big_m))))
            o_ref[0, rows, hs] = hh.astype(o_ref.dtype)

            wk_col = jnp.exp2(gd_col - m_last)
            w_old = jnp.exp2(m_prev - m_last)
            kw = k.astype(F32) * wk_col
            c_refs[h][...] = w_old * cst + _dot_tn(kw.astype(BF16), v)
            n_refs[h][...] = jnp.broadcast_to(w_old * nst + jnp.sum(kw, axis=0, keepdims=True), n_refs[h].shape)
            m_refs[h][...] = jnp.broadcast_to(b_all + m_last, m_refs[h].shape)
        return carry

    lax.fori_loop(0, n_chunks, chunk, 0)


def _mlstm_scan(rev, q, k, v, g, gt):
    b, s, c = q.shape
    ln = SCAN_CHUNK * SCAN_STEP_CHUNKS
    nc = s // ln
    dh = c // M_HEADS
    cidx = (lambda i: nc - 1 - i) if rev else (lambda i: i)
    tile = lambda w: pl.BlockSpec((1, ln, w), lambda bi, i: (bi, cidx(i), 0))
    return pl.pallas_call(
        functools.partial(_mlstm_body, rev),
        grid=(b, nc),
        in_specs=[tile(c), tile(c), tile(c), tile(GATE_LANES),
                  pl.BlockSpec((1, SCAN_STEP_CHUNKS, 4 * M_HEADS, SCAN_CHUNK), lambda bi, i: (bi, cidx(i), 0, 0))],
        out_specs=tile(c),
        out_shape=jax.ShapeDtypeStruct((b, s, c), BF16),
        scratch_shapes=[pltpu.VMEM((dh, dh), F32)] * M_HEADS + [pltpu.VMEM((8, dh), F32)] * M_HEADS
        + [pltpu.VMEM((8, 128), F32)] * M_HEADS,
        compiler_params=_params(2),
        name="mlstm_scan_bwd" if rev else "mlstm_scan_fwd",
    )(q, k, v, g, gt)


def _blockdiag_tiles(w):
    nb, bc, bd = w.shape
    per = PLE_GROUP // bc
    w = w.reshape(nb // per, per, bc, bd)
    eye = jnp.eye(per, dtype=w.dtype)
    dense = jnp.einsum('jncd,nm->jncmd', w, eye)
    return dense.reshape(nb // per, per * bc, per * bd)


def _mlstm_mixer(x, mix_g, w_in, conv_w, conv_b, wq, wk, wv, w_gate, b_gate, head_norm, skip, w_out):
    b, s, d = x.shape
    c = w_out.shape[0]
    wqk =jnp.concatenate([_blockdiag_tiles(wq), _blockdiag_tiles(wk)], axis=-1).astype(BF16)
    wvt = _blockdiag_tiles(wv).astype(BF16)
    ng = w_gate.shape[1]
    wg = jnp.pad(w_gate.reshape(3, c, ng), ((0, 0), (0, 0), (0, GATE_LANES - ng))).astype(BF16)
    bg = jnp.pad(b_gate.reshape(1, ng), ((0, 0), (0, GATE_LANES - ng)))
    k_scale = float(c // M_HEADS) ** -0.5
    q, k, v, xc, z, g, gt = _mlstm_in(x, mix_g, w_in.astype(BF16), conv_w, conv_b, wqk, wvt, wg, bg, k_scale)
    flat = lambda t: t.reshape(b * s, c)
    h_f = _mlstm_scan(False, q, k, v, g, gt)
    h_b = _mlstm_scan(True, q, k, v, g, gt)
    return ("mlstm", (flat(h_f), flat(h_b), head_norm, flat(xc), skip, flat(z), w_out.astype(BF16)),
            (True, True, False, True, False, True, False))


def _hpre_body(x_ref, g_ref, w_ref, bf_ref, lb_ref, q_ref, kf_ref, kb_ref, v_ref, lff_ref, lfb_ref, gg_ref):
    d = x_ref.shape[-1]
    xn = _rms(x_ref[...], g_ref[...]).astype(BF16)
    proj = lambda col: _dot(xn, w_ref[:, col * d:(col + 1) * d])
    q_ref[...] = proj(0).astype(BF16)
    v_ref[...] = proj(3).astype(BF16)
    gg_ref[...] = proj(4).astype(BF16)
    for di, (k_ref, lf_ref) in enumerate(((kf_ref, lff_ref), (kb_ref, lfb_ref))):
        a = proj(1 + di) + bf_ref[di:di + 1, :]
        lb = lb_ref[di:di + 1, :]
        e = jnp.exp(-jnp.abs(a))
        inv = 1.0 / (1.0 + e)
        pos = a >= 0.0
        sig = jnp.where(pos, inv, e * inv)
        nsig = jnp.where(pos, e * inv, inv)
        lf_ref[...] = jnp.log(lb + (1.0 - lb) * sig) * LOG2_E
        k_ref[...] = ((1.0 - lb) * nsig).astype(BF16)


def _hpre(x, g, w_in, b_f, lb):
    t, d = x.shape
    tm = TOKEN_TILE
    row = pl.BlockSpec((tm, d), lambda i: (i, 0))
    shape = lambda dt: jax.ShapeDtypeStruct((t, d), dt)
    return pl.pallas_call(
        _hpre_body,
        grid=(t // tm,),
        in_specs=[row, _const_spec((1, d)), _const_spec(w_in.shape), _const_spec((2, d)), _const_spec((2, d))],
        out_specs=[row] * 7,
        out_shape=[shape(BF16)] * 4 + [shape(F32)] * 2 + [shape(BF16)],
        compiler_params=_params(1),
        name="hgrn_pre",
    )(x, g.reshape(1, d), w_in, b_f, lb)


def _hgrn_body(rev, q_ref, k_ref, v_ref, lf_ref, o_ref, *st_refs):
    nh = len(st_refs)
    ln = SCAN_CHUNK
    n_chunks = q_ref.shape[1] // ln
    dk = q_ref.shape[2] // nh
    hb = ln // 2
    sub = 8
    ng = ln // sub
    qside = 0 if rev else 1
    _zero_rows_at_first_chunk(st_refs, 32)

    r8 = lax.broadcasted_iota(jnp.int32, (ng, sub, dk), 1)
    tt = lax.broadcasted_iota(jnp.int32, (hb, hb), 0)
    ss = lax.broadcasted_iota(jnp.int32, (hb, hb), 1)
    lvl = (lax.bitcast_convert_type((tt ^ ss).astype(F32), jnp.int32) >> 23) - 127
    lvl = jnp.where((ss > tt) if rev else (ss < tt), lvl, jnp.where(tt == ss, -1, -2))
    halves = (slice(0, hb), slice(hb, ln))
    bcast = lambda t, i: jnp.broadcast_to(t[:, i:i + 1, :], t.shape)

    def head(h, rows):
        hs = slice(h * dk, (h + 1) * dk)
        lf = lf_ref[0, rows, hs]
        q, k, v = q_ref[0, rows, hs], k_ref[0, rows, hs], v_ref[0, rows, hs]
        qf, kf = q.astype(F32), k.astype(F32)
        z_up, z_lo = (kf, qf) if rev else (qf, kf)
        a = [jnp.where(lvl == -1, _dot_nt(q[hv], k[hv]), 0.0) for hv in halves]

        lf3 = lf.reshape(ng, sub, dk)
        z3_up, z3_lo = z_up.reshape(ng, sub, dk), z_lo.reshape(ng, sub, dk)
        cc = lf3
        for s in (1, 2, 4):
            if rev:
                cc = cc + jnp.where(r8 < sub - s, pltpu.roll(cc, sub - s, 1), 0.0)
            else:
                cc = cc + jnp.where(r8 >= s, pltpu.roll(cc, s, 1), 0.0)
        for j, m in enumerate((1, 2, 4)):
            upper = (r8 & m) != 0
            ref = m if rev else m - 1
            if m == 1:
                e = jnp.where(upper != rev, lf3, 0.0)
            elif m == 2:
                e = -jnp.abs(cc - jnp.where(r8 < 4, bcast(cc, ref), bcast(cc, ref + 4)))
            else:
                e = -jnp.abs(cc - bcast(cc, ref))
            w = (jnp.where(upper, z3_up, z3_lo) * jnp.exp2(e)).reshape(ln, dk).astype(BF16)
            for i, hv in enumerate(halves):
                a[i] = jnp.where(lvl == j, _dot_nt(w[hv], w[hv]), a[i])
        if rev:
            y = cc
            tot = bcast(cc, 0)
            x = tot - cc
        else:
            x = cc
            tot = bcast(cc, sub - 1)
            y = tot - cc
        x, y, tot = x.reshape(ln, dk), y.reshape(ln, dk), tot.reshape(ln, dk)

        m, j = sub, 3
        while m < ln:
            nb = ln // (2 * m)
            split = lambda t: t.reshape(nb, 2, m, t.shape[-1])
            join = lambda lo, up: jnp.stack([lo, up], axis=1).reshape(ln, dk)
            x4, y4, t4 = split(x), split(y), split(tot)
            w_lo = split(z_lo)[:, 0] * jnp.exp2(y4[:, 0])
            w_up = split(z_up)[:, 1] * jnp.exp2(x4[:, 1])
            if m < hb:
                w = join(w_lo, w_up).astype(BF16)
                wq = (w_lo if rev else w_up).astype(BF16)
                nbh = nb // 2
                mask = lvl.reshape(nbh, 2, m, hb)[:, qside] == j
                for i, hv in enumerate(halves):
                    p = _dot_nt(wq[i * nbh:(i + 1) * nbh].reshape(nbh * m, dk), w[hv])
                    a4 = a[i].reshape(nbh, 2, m, hb)
                    aq = jnp.where(mask, p.reshape(nbh, m, hb), a4[:, qside])
                    parts = [aq, a4[:, 1]] if rev else [a4[:, 0], aq]
                    a[i] = jnp.stack(parts, axis=1).reshape(hb, hb)
            else:
                wl, wu = w_lo.reshape(hb, dk).astype(BF16), w_up.reshape(hb, dk).astype(BF16)
                cross = _dot_nt(wl, wu) if rev else _dot_nt(wu, wl)
            x = join(x4[:, 0], x4[:, 1] + t4[:, 0])
            y = join(y4[:, 0] + t4[:, 1], y4[:, 1])
            tsum = t4[:, 0] + t4[:, 1]
            tot = join(tsum, tsum)
            m, j = 2 * m, j + 1
        e_q, e_k = (y, x) if rev else (x, y)
        st = st_refs[h][...]
        inter = _dot_nt((qf * jnp.exp2(e_q)).astype(BF16), st.astype(BF16))
        a0, a1, cr = a[0].astype(BF16), a[1].astype(BF16), cross.astype(BF16)
        if rev:
            o0 = _dot(jnp.concatenate([a0, cr], axis=1), v)
            o1 = _dot(a1, v[halves[1]])
        else:
            o0 = _dot(a0, v[halves[0]])
            o1 = _dot(jnp.concatenate([cr, a1], axis=1), v)
        out = jnp.concatenate([o0 + inter[halves[0]], o1 + inter[halves[1]]], axis=0)
        o_ref[0, rows, hs] = out.astype(o_ref.dtype)
        kw = (kf * jnp.exp2(e_k)).astype(BF16)
        st_refs[h][...] = jnp.exp2(tot[0:1, :]) * st + _dot_tn(v, kw)

    def chunk(i, carry):
        ck = (n_chunks - 1 - i) if rev else i
        rows = pl.ds(pl.multiple_of(ck * ln, ln), ln)
        for h in range(nh):
            head(h, rows)
        return carry

    lax.fori_loop(0, n_chunks, chunk, 0)


def _hgrn_scan(rev, q, k, v, lf):
    b, s, d = q.shape
    nh = H_HEADS
    dk = d // nh
    ln = SCAN_CHUNK * SCAN_STEP_CHUNKS
    nc = s // ln
    cidx = (lambda i: nc - 1 - i) if rev else (lambda i: i)
    tile = pl.BlockSpec((1, ln, d), lambda bi, i: (bi, cidx(i), 0))
    return pl.pallas_call(
        functools.partial(_hgrn_body, rev),
        grid=(b, nc),
        in_specs=[tile] * 4,
        out_specs=tile,
        out_shape=jax.ShapeDtypeStruct((b, s, d), BF16),
        scratch_shapes=[pltpu.VMEM((dk, dk), F32)] * nh,
        compiler_params=_params(2),
        name="hgrn_scan_bwd" if rev else "hgrn_scan_fwd",
    )(q, k, v, lf)


def _lower_bounds(logits):
    pr = jax.nn.softmax(logits.astype(F32), axis=0)
    return jnp.cumsum(pr, axis=0) - pr[0]


def _hgrn_mixer(x, mix_g, w_in, b_f, lb, head_norm, w_out):
    b, s, d = x.shape
    q, k_f, k_b, v, lf_f, lf_b, gg = _hpre(x.reshape(b * s, d), mix_g, w_in.astype(BF16), b_f, lb)
    seq = lambda t: t.reshape(b, s, d)
    o_f = _hgrn_scan(False, seq(q), seq(k_f), seq(v), seq(lf_f))
    o_b = _hgrn_scan(True, seq(q), seq(k_b), seq(v), seq(lf_b))
    flat = lambda t: t.reshape(b * s, d)
    return "hgrn", (flat(o_f), flat(o_b), head_norm, gg, w_out.astype(BF16)), (True, True, False, True, False)


def _trunk(x, p, ffn1_norm, ffn1_w_in, ffn1_w_out, mix_norm,
           m_w_in, m_conv_w, m_conv_b, m_wq, m_wk, m_wv, m_w_gate, m_b_gate, m_head_norm, m_skip, m_w_out,
           h_w_in, h_b_f, h_lb_logits, h_head_norm, h_w_out,
           ffn2_norm, ffn2_w_in, ffn2_w_out, ple_norm, ple_w_gate, ple_w_proj, final_norm):
    b, s, d = x.shape
    depth = ffn1_norm.shape[0]
    n_mixers = 2
    lb_fwd = _lower_bounds(h_lb_logits[0])
    lb_bwd = _lower_bounds(h_lb_logits[1])
    x = x.reshape(b * s, d)
    for i in range(depth):
        x = _ffn(x, ffn1_norm[i], ffn1_w_in[i].astype(BF16), ffn1_w_out[i].astype(BF16))
        j = i // n_mixers
        xs = x.reshape(b, s, d)
        if i % n_mixers == 0:
            mix = _mlstm_mixer(xs, mix_norm[i], m_w_in[j], m_conv_w[j], m_conv_b[j], m_wq[j], m_wk[j], m_wv[j],
                               m_w_gate[j], m_b_gate[j], m_head_norm[j], m_skip[j], m_w_out[j])
        else:
            lb = jnp.stack([lb_fwd[i], lb_bwd[i]])
            mix = _hgrn_mixer(xs, mix_norm[i], h_w_in[j], h_b_f[j], lb, h_head_norm[j], h_w_out[j])
        ple = (p.reshape(depth, b * s, -1), i, ple_norm[i], ple_w_gate[i].astype(BF16), ple_w_proj[i].astype(BF16))
        x = _ffn(x, ffn2_norm[i], ffn2_w_in[i].astype(BF16), ffn2_w_out[i].astype(BF16), mix=mix, ple=ple,
                 final_g=final_norm if i == depth - 1 else None)
    return x.reshape(b, s, d)


def kernel(x_prompt, x_sample, p_prompt, p_sample, ffn1_norm, ffn1_w_in, ffn1_w_out, mix_norm, m_w_in, m_conv_w, m_conv_b, m_wq, m_wk, m_wv, m_w_gate, m_b_gate, m_head_norm, m_skip, m_w_out, h_w_in, h_b_f, h_lb_logits, h_head_norm, h_w_out, ffn2_norm, ffn2_w_in, ffn2_w_out, ple_norm, ple_w_gate, ple_w_proj, final_norm):
    weights = (ffn1_norm, ffn1_w_in, ffn1_w_out, mix_norm,
               m_w_in, m_conv_w, m_conv_b, m_wq, m_wk, m_wv, m_w_gate, m_b_gate, m_head_norm, m_skip, m_w_out,
               h_w_in, h_b_f, h_lb_logits, h_head_norm, h_w_out,
               ffn2_norm, ffn2_w_in, ffn2_w_out, ple_norm, ple_w_gate, ple_w_proj, final_norm)
    return (_trunk(x_prompt, p_prompt, *weights), _trunk(x_sample, p_sample, *weights))
```

```python
import functools

import jax
import jax.numpy as jnp
from jax import lax
from jax.experimental import pallas as pl
from jax.experimental.pallas import tpu as pltpu

F32 = jnp.float32
BF16 = jnp.bfloat16
EPS = 1e-6
LOG2_E = 1.4426950408889634

M_HEADS = 4
M_QKV_BLOCK = 4
M_CONV = 5
H_HEADS = 8
PLE_GROUP = 256
TOKEN_TILE = 512
SCAN_CHUNK = 256
SCAN_STEP_CHUNKS = 4
CONV_HALO = 8
GATE_LANES = 128
VMEM_LIMIT = 56 * 1024 * 1024
VMEM_LIMIT_MIX_OUT = 62 * 1024 * 1024


def _params(n_axes, vmem_limit=VMEM_LIMIT):
    return pltpu.CompilerParams(dimension_semantics=("arbitrary",) * n_axes,
                                vmem_limit_bytes=vmem_limit)


def _const_spec(shape):
    nd = len(shape)
    return pl.BlockSpec(shape, lambda *_: (0,) * nd, pipeline_mode=pl.Buffered(1))


def _rms(x, g):
    return x * lax.rsqrt(jnp.mean(x * x, axis=-1, keepdims=True) + EPS) * g


def _sigmoid(x):
    return 1.0 / (1.0 + jnp.exp(-x))


def _dot(a, b):
    return jnp.dot(a, b, preferred_element_type=F32)


def _dot_nt(a, b):
    return lax.dot_general(a, b, (((1,), (1,)), ((), ())), preferred_element_type=F32)


def _dot_tn(a, b):
    return lax.dot_general(a, b, (((0,), (0,)), ((), ())), preferred_element_type=F32)


def _mix_out(n_heads, ha_ref, hb_ref, hn_ref, add_ref, addw_ref, gate_ref, wo_ref, gate_fn):
    dh = ha_ref.shape[-1] // n_heads
    parts = []
    for h in range(n_heads):
        hs = slice(h * dh, (h + 1) * dh)
        hsum = ha_ref[:, hs].astype(F32) + hb_ref[:, hs].astype(F32)
        y = hsum * lax.rsqrt(jnp.mean(hsum * hsum, axis=-1, keepdims=True) + EPS) * hn_ref[:, hs]
        if add_ref is not None:
            y = y + addw_ref[:, hs] * add_ref[:, hs].astype(F32)
        parts.append((y * gate_fn(gate_ref[:, hs].astype(F32))).astype(BF16))
    return _dot(jnp.concatenate(parts, axis=-1), wo_ref[...])


def _ffn_body(mix, has_ple, has_final, *refs):
    it = iter(refs)
    x = next(it)[...]
    if mix == "mlstm":
        ha, hb, hn, xc, sk, z, wo = (next(it) for _ in range(7))
        x = x + _mix_out(M_HEADS, ha, hb, hn, xc, sk, z, wo, lambda t: t * _sigmoid(t))
    elif mix == "hgrn":
        ha, hb, hn, gg, wo = (next(it) for _ in range(5))
        x = x + _mix_out(H_HEADS, ha, hb, hn, None, None, gg, wo, _sigmoid)
    g_ref, win_ref, wout_ref = (next(it) for _ in range(3))
    if has_ple:
        p_ref, pg_ref, pwg_ref, pwp_ref = (next(it) for _ in range(4))
    if has_final:
        fg_ref = next(it)
    o_ref = next(it)
    d_ff = wout_ref.shape[0]
    xn = _rms(x, g_ref[...]).astype(BF16)
    a = _dot(xn, win_ref[:, :d_ff])
    u = _dot(xn, win_ref[:, d_ff:])
    act = (a * _sigmoid(a) * u).astype(BF16)
    x = x + 0.5 * _dot(act, wout_ref[...])
    if has_ple:
        xg = _rms(x, pg_ref[...]).astype(BF16)
        gate = _sigmoid(_dot(xg, pwg_ref[...]))
        x = x + gate * _dot(p_ref[...].astype(BF16), pwp_ref[...])
    if has_final:
        x = _rms(x, fg_ref[...])
    o_ref[...] = x


def _ffn(x, g, w_in, w_out, mix=None, ple=None, final_g=None):
    t, d = x.shape
    tm = TOKEN_TILE
    row = lambda w: pl.BlockSpec((tm, w), lambda i: (i, 0))
    args, specs = [x], [row(d)]
    kind = None
    if mix is not None:
        kind, operands, per_token = mix
        for item, tok in zip(operands, per_token):
            if item.ndim == 1:
                item = item.reshape(1, -1)
            args.append(item)
            specs.append(row(item.shape[1]) if tok else _const_spec(item.shape))
    args += [g.reshape(1, d), w_in, w_out]
    specs += [_const_spec((1, d)), _const_spec(w_in.shape), _const_spec(w_out.shape)]
    if ple is not None:
        p, layer, pg, pwg, pwp = ple
        args += [p, pg.reshape(1, d), pwg, pwp]
        specs += [pl.BlockSpec((None, tm, p.shape[2]), lambda i: (layer, i, 0)),
                  _const_spec((1, d)), _const_spec(pwg.shape), _const_spec(pwp.shape)]
    if final_g is not None:
        args.append(final_g.reshape(1, d))
        specs.append(_const_spec((1, d)))
    return pl.pallas_call(
        functools.partial(_ffn_body, kind, ple is not None, final_g is not None),
        grid=(t // tm,),
        in_specs=specs,
        out_specs=row(d),
        out_shape=jax.ShapeDtypeStruct((t, d), F32),
        compiler_params=_params(1, VMEM_LIMIT if mix is None else VMEM_LIMIT_MIX_OUT),
        name="ffn" if mix is None else "mix_out_ffn",
    )(*args)


def _log_sigmoid(x):
    return jnp.minimum(x, 0.0) - jnp.log1p(jnp.exp(-jnp.abs(x)))


def _min_body(k_scale, x_ref, xprev_ref, xnext_ref, ng_ref, win_ref, cw_ref, cb_ref, wqk_ref, wv_ref, wg_ref,
              bg_ref, q_ref, k_ref, v_ref, xc_ref, z_ref, g_ref, gt_ref, ext_ref):
    i = pl.program_id(1)
    last = pl.num_programs(1) - 1
    ts = x_ref.shape[1]
    c = q_ref.shape[2]
    hl = CONV_HALO
    pad = M_CONV // 2
    ng = ng_ref[...]
    xn = _rms(x_ref[0], ng)
    xn_ext = jnp.concatenate([_rms(xprev_ref[0], ng), xn, _rms(xnext_ref[0], ng)], axis=0).astype(BF16)
    xn = xn.astype(BF16)
    z_ref[0] = _dot(xn, win_ref[:, c:]).astype(BF16)
    xm_ext = _dot(xn_ext, win_ref[:, :c])
    r = lax.broadcasted_iota(jnp.int32, (ts + 2 * hl, 1), 0)
    inside = ((r >= hl) | (i > 0)) & ((r < hl + ts) | (i < last))
    ext_ref[...] = jnp.where(inside, xm_ext, 0.0)
    xm = ext_ref[hl:hl + ts, :].astype(BF16)
    acc = jnp.broadcast_to(cb_ref[...], (ts, c))
    for j in range(M_CONV):
        acc = acc + cw_ref[j:j + 1, :] * ext_ref[hl - pad + j:hl - pad + j + ts, :]
    xc = (acc * _sigmoid(acc)).astype(BF16)
    xc_ref[0] = xc
    gw = PLE_GROUP
    gates = jnp.broadcast_to(bg_ref[...], (ts, GATE_LANES))
    for j in range(c // gw):
        cs = slice(j * gw, (j + 1) * gw)
        qk = _dot(xc[:, cs], wqk_ref[j])
        qj = qk[:, :gw].astype(BF16)
        kj = qk[:, gw:]
        kjb = kj.astype(BF16)
        vj = _dot(xm[:, cs], wv_ref[j]).astype(BF16)
        q_ref[0, :, cs] = qj
        k_ref[0, :, cs] = (kj * k_scale).astype(BF16)
        v_ref[0, :, cs] = vj
        gates = gates + _dot(qj, wg_ref[0, cs, :]) + _dot(kjb, wg_ref[1, cs, :]) + _dot(vj, wg_ref[2, cs, :])
    col = lax.broadcasted_iota(jnp.int32, gates.shape, 1)
    is_forget = (col % (2 * M_HEADS)) >= M_HEADS
    gates = jnp.where(is_forget, _log_sigmoid(gates), gates)
    g_ref[0] = gates
    gates_t = gates.T
    for ci in range(ts // SCAN_CHUNK):
        gt_ref[0, ci] = gates_t[:4 * M_HEADS, ci * SCAN_CHUNK:(ci + 1) * SCAN_CHUNK]


def _mlstm_in(x, norm_g, w_in, conv_w, conv_b, wqk, wv, wg, bg, k_scale):
    b, s, d = x.shape
    c = w_in.shape[1] // 2
    ts = TOKEN_TILE
    hb = ts // CONV_HALO
    nh_blocks = s // CONV_HALO
    tile = lambda w: pl.BlockSpec((1, ts, w), lambda bi, i: (bi, i, 0))
    return pl.pallas_call(
        functools.partial(_min_body, k_scale),
        grid=(b, s // ts),
        in_specs=[
            tile(d),
            pl.BlockSpec((1, CONV_HALO, d), lambda bi, i: (bi, jnp.maximum(i * hb - 1, 0), 0)),
            pl.BlockSpec((1, CONV_HALO, d), lambda bi, i: (bi, jnp.minimum((i + 1) * hb, nh_blocks - 1), 0)),
            _const_spec((1, d)), _const_spec(w_in.shape), _const_spec(conv_w.shape), _const_spec((1, c)),
            _const_spec(wqk.shape), _const_spec(wv.shape), _const_spec(wg.shape), _const_spec(bg.shape),
        ],
        out_specs=[tile(c)] * 5 + [tile(GATE_LANES),
                                   pl.BlockSpec((1, ts // SCAN_CHUNK, 4 * M_HEADS, SCAN_CHUNK),
                                                lambda bi, i: (bi, i, 0, 0))],
        out_shape=[jax.ShapeDtypeStruct((b, s, c), BF16)] * 5
        + [jax.ShapeDtypeStruct((b, s, GATE_LANES), F32),
           jax.ShapeDtypeStruct((b, s // SCAN_CHUNK, 4 * M_HEADS, SCAN_CHUNK), F32)],
        scratch_shapes=[pltpu.VMEM((ts + 2 * CONV_HALO, c), F32)],
        compiler_params=_params(2),
        name="mlstm_in",
    )(x, x, x, norm_g.reshape(1, d), w_in, conv_w, conv_b.reshape(1, c), wqk, wv, wg, bg)


def _running(x, axis, rev, op, fill):
    n = x.shape[axis]
    idx = lax.broadcasted_iota(jnp.int32, x.shape, axis)
    s = 1
    while s < n:
        if rev:
            x = op(x, jnp.where(idx < n - s, pltpu.roll(x, n - s, axis), fill))
        else:
            x = op(x, jnp.where(idx >= s, pltpu.roll(x, s, axis), fill))
        s *= 2
    return x


def _zero_rows_at_first_chunk(refs, rows_per_trip):
    rows = refs[0].shape[0]

    def zero(i, carry):
        r0 = pl.multiple_of(i * rows_per_trip, rows_per_trip)
        for ref in refs:
            ref[pl.ds(r0, rows_per_trip), :] = jnp.zeros((rows_per_trip, ref.shape[1]), F32)
        return carry

    lax.fori_loop(0, jnp.where(pl.program_id(1) == 0, rows // rows_per_trip, 0), zero, 0)


def _mlstm_body(rev, q_ref, k_ref, v_ref, g_ref, gt_ref, o_ref, *state_refs):
    nh = M_HEADS
    c_refs, n_refs, m_refs = state_refs[:nh], state_refs[nh:2 * nh], state_refs[2 * nh:]
    ln = SCAN_CHUNK
    n_chunks = q_ref.shape[1] // ln
    dh = q_ref.shape[2] // nh
    _zero_rows_at_first_chunk(c_refs, 64)
    _zero_rows_at_first_chunk(n_refs + m_refs, 8)

    tt = lax.broadcasted_iota(jnp.int32, (ln, ln), 0)
    ss = lax.broadcasted_iota(jnp.int32, (ln, ln), 1)
    valid = (ss >= tt) if rev else (ss <= tt)
    end = 0 if rev else ln - 1
    goff = 2 * nh if rev else 0

    def chunk(i, carry):
        ck = (n_chunks - 1 - i) if rev else i
        rows = pl.ds(pl.multiple_of(ck * ln, ln), ln)
        g = g_ref[0, rows, :] * LOG2_E
        gt = gt_ref[0, ck] * LOG2_E
        bcol = _running(g, 0, rev, jnp.add, 0.0)
        brow = _running(gt, 1, rev, jnp.add, 0.0)
        gdcol = g - pltpu.roll(bcol, g.shape[1] - nh, 1)
        gdrow = gt - pltpu.roll(brow, gt.shape[0] - nh, 0)
        mcol = _running(gdcol, 0, rev, jnp.maximum, -jnp.inf)
        for h in range(nh):
            hs = slice(h * dh, (h + 1) * dh)
            ci, cf = goff + h, goff + nh + h
            gd_row, gd_col = gdrow[ci:ci + 1, :], gdcol[:, ci:ci + 1]
            b_col = bcol[:, cf:cf + 1]
            b_all = b_col[end:end + 1, :]
            m_prev = m_refs[h][0:1, 0:1]
            big_m = jnp.maximum(mcol[:, ci:ci + 1], m_prev)
            m_last = big_m[end:end + 1, :]
            q, k, v = q_ref[0, rows, hs], k_ref[0, rows, hs], v_ref[0, rows, hs]

            s_mat = _dot_nt(q, k) * jnp.exp2(jnp.where(valid, gd_row - big_m, -jnp.inf))
            w_inter = jnp.exp2(m_prev - big_m)
            cst = c_refs[h][...]
            nst = n_refs[h][0:1, :]
            num = _dot(s_mat.astype(BF16), v) + w_inter * _dot(q, cst.astype(BF16))
            den = jnp.sum(s_mat, axis=-1, keepdims=True) + \
                w_inter * jnp.sum(q.astype(F32) * nst, axis=-1, keepdims=True)
            hh = num * (1.0 / jnp.maximum(jnp.abs(den), jnp.exp2(-(b_col + big_m))))
            o_ref[0, rows, hs] = hh.astype(o_ref.dtype)

            wk_col = jnp.exp2(gd_col - m_last)
            w_old = jnp.exp2(m_prev - m_last)
            kw = k.astype(F32) * wk_col
            c_refs[h][...] = w_old * cst + _dot_tn(kw.astype(BF16), v)
            n_refs[h][...] = jnp.broadcast_to(w_old * nst + jnp.sum(kw, axis=0, keepdims=True), n_refs[h].shape)
            m_refs[h][...] = jnp.broadcast_to(b_all + m_last, m_refs[h].shape)
        return carry

    lax.fori_loop(0, n_chunks, chunk, 0)


def _mlstm_scan(rev, q, k, v, g, gt):
    b, s, c = q.shape
    ln = SCAN_CHUNK * SCAN_STEP_CHUNKS
    nc = s // ln
    dh = c // M_HEADS
    cidx = (lambda i: nc - 1 - i) if rev else (lambda i: i)
    tile = lambda w: pl.BlockSpec((1, ln, w), lambda bi, i: (bi, cidx(i), 0))
    return pl.pallas_call(
        functools.partial(_mlstm_body, rev),
        grid=(b, nc),
        in_specs=[tile(c), tile(c), tile(c), tile(GATE_LANES),
                  pl.BlockSpec((1, SCAN_STEP_CHUNKS, 4 * M_HEADS, SCAN_CHUNK), lambda bi, i: (bi, cidx(i), 0, 0))],
        out_specs=tile(c),
        out_shape=jax.ShapeDtypeStruct((b, s, c), BF16),
        scratch_shapes=[pltpu.VMEM((dh, dh), F32)] * M_HEADS + [pltpu.VMEM((8, dh), F32)] * M_HEADS
        + [pltpu.VMEM((8, 128), F32)] * M_HEADS,
        compiler_params=_params(2),
        name="mlstm_scan_bwd" if rev else "mlstm_scan_fwd",
    )(q, k, v, g, gt)


def _blockdiag_tiles(w):
    nb, bc, bd = w.shape
    per = PLE_GROUP // bc
    w = w.reshape(nb // per, per, bc, bd)
    eye = jnp.eye(per, dtype=w.dtype)
    dense = jnp.einsum('jncd,nm->jncmd', w, eye)
    return dense.reshape(nb // per, per * bc, per * bd)


def _mlstm_mixer(x, mix_g, w_in, conv_w, conv_b, wq, wk, wv, w_gate, b_gate, head_norm, skip, w_out):
    b, s, d = x.shape
    c = w_out.shape[0]
    wqk =jnp.concatenate([_blockdiag_tiles(wq), _blockdiag_tiles(wk)], axis=-1).astype(BF16)
    wvt = _blockdiag_tiles(wv).astype(BF16)
    ng = w_gate.shape[1]
    wg = jnp.pad(w_gate.reshape(3, c, ng), ((0, 0), (0, 0), (0, GATE_LANES - ng))).astype(BF16)
    bg = jnp.pad(b_gate.reshape(1, ng), ((0, 0), (0, GATE_LANES - ng)))
    k_scale = float(c // M_HEADS) ** -0.5
    q, k, v, xc, z, g, gt = _mlstm_in(x, mix_g, w_in.astype(BF16), conv_w, conv_b, wqk, wvt, wg, bg, k_scale)
    flat = lambda t: t.reshape(b * s, c)
    h_f = _mlstm_scan(False, q, k, v, g, gt)
    h_b = _mlstm_scan(True, q, k, v, g, gt)
    return ("mlstm", (flat(h_f), flat(h_b), head_norm, flat(xc), skip, flat(z), w_out.astype(BF16)),
            (True, True, False, True, False, True, False))


def _hpre_body(x_ref, g_ref, w_ref, bf_ref, lb_ref, q_ref, kf_ref, kb_ref, v_ref, lff_ref, lfb_ref, gg_ref):
    d = x_ref.shape[-1]
    xn = _rms(x_ref[...], g_ref[...]).astype(BF16)
    proj = lambda col: _dot(xn, w_ref[:, col * d:(col + 1) * d])
    q_ref[...] = proj(0).astype(BF16)
    v_ref[...] = proj(3).astype(BF16)
    gg_ref[...] = proj(4).astype(BF16)
    for di, (k_ref, lf_ref) in enumerate(((kf_ref, lff_ref), (kb_ref, lfb_ref))):
        a = proj(1 + di) + bf_ref[di:di + 1, :]
        lb = lb_ref[di:di + 1, :]
        e = jnp.exp(-jnp.abs(a))
        inv = 1.0 / (1.0 + e)
        pos = a >= 0.0
        sig = jnp.where(pos, inv, e * inv)
        nsig = jnp.where(pos, e * inv, inv)
        lf_ref[...] = jnp.log(lb + (1.0 - lb) * sig) * LOG2_E
        k_ref[...] = ((1.0 - lb) * nsig).astype(BF16)


def _hpre(x, g, w_in, b_f, lb):
    t, d = x.shape
    tm = TOKEN_TILE
    row = pl.BlockSpec((tm, d), lambda i: (i, 0))
    shape = lambda dt: jax.ShapeDtypeStruct((t, d), dt)
    return pl.pallas_call(
        _hpre_body,
        grid=(t // tm,),
        in_specs=[row, _const_spec((1, d)), _const_spec(w_in.shape), _const_spec((2, d)), _const_spec((2, d))],
        out_specs=[row] * 7,
        out_shape=[shape(BF16)] * 4 + [shape(F32)] * 2 + [shape(BF16)],
        compiler_params=_params(1),
        name="hgrn_pre",
    )(x, g.reshape(1, d), w_in, b_f, lb)


def _hgrn_body(rev, q_ref, k_ref, v_ref, lf_ref, o_ref, *st_refs):
    nh = len(st_refs)
    ln = SCAN_CHUNK
    n_chunks = q_ref.shape[1] // ln
    dk = q_ref.shape[2] // nh
    hb = ln // 2
    sub = 8
    ng = ln // sub
    qside = 0 if rev else 1
    _zero_rows_at_first_chunk(st_refs, 32)

    r8 = lax.broadcasted_iota(jnp.int32, (ng, sub, dk), 1)
    tt = lax.broadcasted_iota(jnp.int32, (hb, hb), 0)
    ss = lax.broadcasted_iota(jnp.int32, (hb, hb), 1)
    lvl = (lax.bitcast_convert_type((tt ^ ss).astype(F32), jnp.int32) >> 23) - 127
    lvl = jnp.where((ss > tt) if rev else (ss < tt), lvl, jnp.where(tt == ss, -1, -2))
    halves = (slice(0, hb), slice(hb, ln))
    bcast = lambda t, i: jnp.broadcast_to(t[:, i:i + 1, :], t.shape)

    def head(h, rows):
        hs = slice(h * dk, (h + 1) * dk)
        lf = lf_ref[0, rows, hs]
        q, k, v = q_ref[0, rows, hs], k_ref[0, rows, hs], v_ref[0, rows, hs]
        qf, kf = q.astype(F32), k.astype(F32)
        z_up, z_lo = (kf, qf) if rev else (qf, kf)
        a = [jnp.where(lvl == -1, _dot_nt(q[hv], k[hv]), 0.0) for hv in halves]

        lf3 = lf.reshape(ng, sub, dk)
        z3_up, z3_lo = z_up.reshape(ng, sub, dk), z_lo.reshape(ng, sub, dk)
        cc = lf3
        for s in (1, 2, 4):
            if rev:
                cc = cc + jnp.where(r8 < sub - s, pltpu.roll(cc, sub - s, 1), 0.0)
            else:
                cc = cc + jnp.where(r8 >= s, pltpu.roll(cc, s, 1), 0.0)
        for j, m in enumerate((1, 2, 4)):
            upper = (r8 & m) != 0
            ref = m if rev else m - 1
            if m == 1:
                e = jnp.where(upper != rev, lf3, 0.0)
            elif m == 2:
                e = -jnp.abs(cc - jnp.where(r8 < 4, bcast(cc, ref), bcast(cc, ref + 4)))
            else:
                e = -jnp.abs(cc - bcast(cc, ref))
            w = (jnp.where(upper, z3_up, z3_lo) * jnp.exp2(e)).reshape(ln, dk).astype(BF16)
            for i, hv in enumerate(halves):
                a[i] = jnp.where(lvl == j, _dot_nt(w[hv], w[hv]), a[i])
        if rev:
            y = cc
            tot = bcast(cc, 0)
            x = tot - cc
        else:
            x = cc
            tot = bcast(cc, sub - 1)
            y = tot - cc
        x, y, tot = x.reshape(ln, dk), y.reshape(ln, dk), tot.reshape(ln, dk)

        m, j = sub, 3
        while m < ln:
            nb = ln // (2 * m)
            split = lambda t: t.reshape(nb, 2, m, t.shape[-1])
            join = lambda lo, up: jnp.stack([lo, up], axis=1).reshape(ln, dk)
            x4, y4, t4 = split(x), split(y), split(tot)
            w_lo = split(z_lo)[:, 0] * jnp.exp2(y4[:, 0])
            w_up = split(z_up)[:, 1] * jnp.exp2(x4[:, 1])
            if m < hb:
                w = join(w_lo, w_up).astype(BF16)
                wq = (w_lo if rev else w_up).astype(BF16)
                nbh = nb // 2
                mask = lvl.reshape(nbh, 2, m, hb)[:, qside] == j
                for i, hv in enumerate(halves):
                    p = _dot_nt(wq[i * nbh:(i + 1) * nbh].reshape(nbh * m, dk), w[hv])
                    a4 = a[i].reshape(nbh, 2, m, hb)
                    aq = jnp.where(mask, p.reshape(nbh, m, hb), a4[:, qside])
                    parts = [aq, a4[:, 1]] if rev else [a4[:, 0], aq]
                    a[i] = jnp.stack(parts, axis=1).reshape(hb, hb)
            else:
                wl, wu = w_lo.reshape(hb, dk).astype(BF16), w_up.reshape(hb, dk).astype(BF16)
                cross = _dot_nt(wl, wu) if rev else _dot_nt(wu, wl)
            x = join(x4[:, 0], x4[:, 1] + t4[:, 0])
            y = join(y4[:, 0] + t4[:, 1], y4[:, 1])
            tsum = t4[:, 0] + t4[:, 1]
            tot = join(tsum, tsum)
            m, j = 2 * m, j + 1
        e_q, e_k = (y, x) if rev else (x, y)
        st = st_refs[h][...]
        inter = _dot_nt((qf * jnp.exp2(e_q)).astype(BF16), st.astype(BF16))
        a0, a1, cr = a[0].astype(BF16), a[1].astype(BF16), cross.astype(BF16)
        if rev:
            o0 = _dot(jnp.concatenate([a0, cr], axis=1), v)
            o1 = _dot(a1, v[halves[1]])
        else:
            o0 = _dot(a0, v[halves[0]])
            o1 = _dot(jnp.concatenate([cr, a1], axis=1), v)
        out = jnp.concatenate([o0 + inter[halves[0]], o1 + inter[halves[1]]], axis=0)
        o_ref[0, rows, hs] = out.astype(o_ref.dtype)
        kw = (kf * jnp.exp2(e_k)).astype(BF16)
        st_refs[h][...] = jnp.exp2(tot[0:1, :]) * st + _dot_tn(v, kw)

    def chunk(i, carry):
        ck = (n_chunks - 1 - i) if rev else i
        rows = pl.ds(pl.multiple_of(ck * ln, ln), ln)
        for h in range(nh):
            head(h, rows)
        return carry

    lax.fori_loop(0, n_chunks, chunk, 0)


def _hgrn_scan(rev, q, k, v, lf):
    b, s, d = q.shape
    nh = H_HEADS
    dk = d // nh
    ln = SCAN_CHUNK * SCAN_STEP_CHUNKS
    nc = s // ln
    cidx = (lambda i: nc - 1 - i) if rev else (lambda i: i)
    tile = pl.BlockSpec((1, ln, d), lambda bi, i: (bi, cidx(i), 0))
    return pl.pallas_call(
        functools.partial(_hgrn_body, rev),
        grid=(b, nc),
        in_specs=[tile] * 4,
        out_specs=tile,
        out_shape=jax.ShapeDtypeStruct((b, s, d), BF16),
        scratch_shapes=[pltpu.VMEM((dk, dk), F32)] * nh,
        compiler_params=_params(2),
        name="hgrn_scan_bwd" if rev else "hgrn_scan_fwd",
    )(q, k, v, lf)


def _lower_bounds(logits):
    pr = jax.nn.softmax(logits.astype(F32), axis=0)
    return jnp.cumsum(pr, axis=0) - pr[0]


def _hgrn_mixer(x, mix_g, w_in, b_f, lb, head_norm, w_out):
    b, s, d = x.shape
    q, k_f, k_b, v, lf_f, lf_b, gg = _hpre(x.reshape(b * s, d), mix_g, w_in.astype(BF16), b_f, lb)
    seq = lambda t: t.reshape(b, s, d)
    o_f = _hgrn_scan(False, seq(q), seq(k_f), seq(v), seq(lf_f))
    o_b = _hgrn_scan(True, seq(q), seq(k_b), seq(v), seq(lf_b))
    flat = lambda t: t.reshape(b * s, d)
    return "hgrn", (flat(o_f), flat(o_b), head_norm, gg, w_out.astype(BF16)), (True, True, False, True, False)


def _trunk(x, p, ffn1_norm, ffn1_w_in, ffn1_w_out, mix_norm,
           m_w_in, m_conv_w, m_conv_b, m_wq, m_wk, m_wv, m_w_gate, m_b_gate, m_head_norm, m_skip, m_w_out,
           h_w_in, h_b_f, h_lb_logits, h_head_norm, h_w_out,
           ffn2_norm, ffn2_w_in, ffn2_w_out, ple_norm, ple_w_gate, ple_w_proj, final_norm):
    b, s, d = x.shape
    depth = ffn1_norm.shape[0]
    n_mixers = 2
    lb_fwd = _lower_bounds(h_lb_logits[0])
    lb_bwd = _lower_bounds(h_lb_logits[1])
    x = x.reshape(b * s, d)
    for i in range(depth):
        x = _ffn(x, ffn1_norm[i], ffn1_w_in[i].astype(BF16), ffn1_w_out[i].astype(BF16))
        j = i // n_mixers
        xs = x.reshape(b, s, d)
        if i % n_mixers == 0:
            mix = _mlstm_mixer(xs, mix_norm[i], m_w_in[j], m_conv_w[j], m_conv_b[j], m_wq[j], m_wk[j], m_wv[j],
                               m_w_gate[j], m_b_gate[j], m_head_norm[j], m_skip[j], m_w_out[j])
        else:
            lb = jnp.stack([lb_fwd[i], lb_bwd[i]])
            mix = _hgrn_mixer(xs, mix_norm[i], h_w_in[j], h_b_f[j], lb, h_head_norm[j], h_w_out[j])
        ple = (p.reshape(depth, b * s, -1), i, ple_norm[i], ple_w_gate[i].astype(BF16), ple_w_proj[i].astype(BF16))
        x = _ffn(x, ffn2_norm[i], ffn2_w_in[i].astype(BF16), ffn2_w_out[i].astype(BF16), mix=mix, ple=ple,
                 final_g=final_norm if i == depth - 1 else None)
    return x.reshape(b, s, d)


def kernel(x_prompt, x_sample, p_prompt, p_sample, ffn1_norm, ffn1_w_in, ffn1_w_out, mix_norm, m_w_in, m_conv_w, m_conv_b, m_wq, m_wk, m_wv, m_w_gate, m_b_gate, m_head_norm, m_skip, m_w_out, h_w_in, h_b_f, h_lb_logits, h_head_norm, h_w_out, ffn2_norm, ffn2_w_in, ffn2_w_out, ple_norm, ple_w_gate, ple_w_proj, final_norm):
    weights = (ffn1_norm, ffn1_w_in, ffn1_w_out, mix_norm,
               m_w_in, m_conv_w, m_conv_b, m_wq, m_wk, m_wv, m_w_gate, m_b_gate, m_head_norm, m_skip, m_w_out,
               h_w_in, h_b_f, h_lb_logits, h_head_norm, h_w_out,
               ffn2_norm, ffn2_w_in, ffn2_w_out, ple_norm, ple_w_gate, ple_w_proj, final_norm)
    return (_trunk(x_prompt, p_prompt, *weights), _trunk(x_sample, p_sample, *weights))
```

```python
import functools

import jax
import jax.numpy as jnp
from jax import lax
from jax.experimental import pallas as pl
from jax.experimental.pallas import tpu as pltpu

F32 = jnp.float32
BF16 = jnp.bfloat16
EPS = 1e-6
LOG2_E = 1.4426950408889634

M_HEADS = 4
M_QKV_BLOCK = 4
M_CONV = 5
H_HEADS = 8
PLE_GROUP = 256
TOKEN_TILE = 512
SCAN_CHUNK = 256
SCAN_STEP_CHUNKS = 4
CONV_HALO = 8
GATE_LANES = 128
VMEM_LIMIT = 56 * 1024 * 1024
VMEM_LIMIT_MIX_OUT = 62 * 1024 * 1024


def _params(n_axes, vmem_limit=VMEM_LIMIT):
    return pltpu.CompilerParams(dimension_semantics=("arbitrary",) * n_axes,
                                vmem_limit_bytes=vmem_limit)


def _const_spec(shape):
    nd = len(shape)
    return pl.BlockSpec(shape, lambda *_: (0,) * nd, pipeline_mode=pl.Buffered(1))


def _rms(x, g):
    return x * lax.rsqrt(jnp.mean(x * x, axis=-1, keepdims=True) + EPS) * g


def _sigmoid(x):
    return 0.5 * jnp.tanh(0.5 * x) + 0.5


def _dot(a, b):
    return jnp.dot(a, b, preferred_element_type=F32)


def _dot_nt(a, b):
    return lax.dot_general(a, b, (((1,), (1,)), ((), ())), preferred_element_type=F32)


def _dot_tn(a, b):
    return lax.dot_general(a, b, (((0,), (0,)), ((), ())), preferred_element_type=F32)


def _mix_out(n_heads, ha_ref, hb_ref, hn_ref, add_ref, addw_ref, gate_ref, wo_ref, gate_fn):
    dh = ha_ref.shape[-1] // n_heads
    parts = []
    for h in range(n_heads):
        hs = slice(h * dh, (h + 1) * dh)
        hsum = ha_ref[:, hs].astype(F32) + hb_ref[:, hs].astype(F32)
        y = hsum * lax.rsqrt(jnp.mean(hsum * hsum, axis=-1, keepdims=True) + EPS) * hn_ref[:, hs]
        if add_ref is not None:
            y = y + addw_ref[:, hs] * add_ref[:, hs].astype(F32)
        parts.append((y * gate_fn(gate_ref[:, hs].astype(F32))).astype(BF16))
    return _dot(jnp.concatenate(parts, axis=-1), wo_ref[...])


def _ffn_body(mix, has_ple, has_final, *refs):
    it = iter(refs)
    x = next(it)[...]
    if mix == "mlstm":
        ha, hb, hn, xc, sk, z, wo = (next(it) for _ in range(7))
        x = x + _mix_out(M_HEADS, ha, hb, hn, xc, sk, z, wo, lambda t: t * _sigmoid(t))
    elif mix == "hgrn":
        ha, hb, hn, gg, wo = (next(it) for _ in range(5))
        x = x + _mix_out(H_HEADS, ha, hb, hn, None, None, gg, wo, _sigmoid)
    g_ref, win_ref, wout_ref = (next(it) for _ in range(3))
    if has_ple:
        p_ref, pg_ref, pwg_ref, pwp_ref = (next(it) for _ in range(4))
    if has_final:
        fg_ref = next(it)
    o_ref = next(it)
    d_ff = wout_ref.shape[0]
    xn = _rms(x, g_ref[...]).astype(BF16)
    a = _dot(xn, win_ref[:, :d_ff])
    u = _dot(xn, win_ref[:, d_ff:])
    act = (a * _sigmoid(a) * u).astype(BF16)
    x = x + 0.5 * _dot(act, wout_ref[...])
    if has_ple:
        xg = _rms(x, pg_ref[...]).astype(BF16)
        gate = _sigmoid(_dot(xg, pwg_ref[...]))
        x = x + gate * _dot(p_ref[...].astype(BF16), pwp_ref[...])
    if has_final:
        x = _rms(x, fg_ref[...])
    o_ref[...] = x


def _ffn(x, g, w_in, w_out, mix=None, ple=None, final_g=None):
    t, d = x.shape
    tm = TOKEN_TILE
    row = lambda w: pl.BlockSpec((tm, w), lambda i: (i, 0))
    args, specs = [x], [row(d)]
    kind = None
    if mix is not None:
        kind, operands, per_token = mix
        for item, tok in zip(operands, per_token):
            if item.ndim == 1:
                item = item.reshape(1, -1)
            args.append(item)
            specs.append(row(item.shape[1]) if tok else _const_spec(item.shape))
    args += [g.reshape(1, d), w_in, w_out]
    specs += [_const_spec((1, d)), _const_spec(w_in.shape), _const_spec(w_out.shape)]
    if ple is not None:
        p, layer, pg, pwg, pwp = ple
        args += [p, pg.reshape(1, d), pwg, pwp]
        specs += [pl.BlockSpec((None, tm, p.shape[2]), lambda i: (layer, i, 0)),
                  _const_spec((1, d)), _const_spec(pwg.shape), _const_spec(pwp.shape)]
    if final_g is not None:
        args.append(final_g.reshape(1, d))
        specs.append(_const_spec((1, d)))
    return pl.pallas_call(
        functools.partial(_ffn_body, kind, ple is not None, final_g is not None),
        grid=(t // tm,),
        in_specs=specs,
        out_specs=row(d),
        out_shape=jax.ShapeDtypeStruct((t, d), F32),
        compiler_params=_params(1, VMEM_LIMIT if mix is None else VMEM_LIMIT_MIX_OUT),
        name="ffn" if mix is None else "mix_out_ffn",
    )(*args)


def _log_sigmoid(x):
    return jnp.minimum(x, 0.0) - jnp.log1p(jnp.exp(-jnp.abs(x)))


def _min_body(k_scale, x_ref, xprev_ref, xnext_ref, ng_ref, win_ref, cw_ref, cb_ref, wqk_ref, wv_ref, wg_ref,
              bg_ref, q_ref, k_ref, v_ref, xc_ref, z_ref, g_ref, gt_ref, ext_ref):
    i = pl.program_id(1)
    last = pl.num_programs(1) - 1
    ts = x_ref.shape[1]
    c = q_ref.shape[2]
    hl = CONV_HALO
    pad = M_CONV // 2
    ng = ng_ref[...]
    xn = _rms(x_ref[0], ng)
    xn_ext = jnp.concatenate([_rms(xprev_ref[0], ng), xn, _rms(xnext_ref[0], ng)], axis=0).astype(BF16)
    xn = xn.astype(BF16)
    z_ref[0] = _dot(xn, win_ref[:, c:]).astype(BF16)
    xm_ext = _dot(xn_ext, win_ref[:, :c])
    r = lax.broadcasted_iota(jnp.int32, (ts + 2 * hl, 1), 0)
    inside = ((r >= hl) | (i > 0)) & ((r < hl + ts) | (i < last))
    ext_ref[...] = jnp.where(inside, xm_ext, 0.0)
    xm = ext_ref[hl:hl + ts, :].astype(BF16)
    acc = jnp.broadcast_to(cb_ref[...], (ts, c))
    for j in range(M_CONV):
        acc = acc + cw_ref[j:j + 1, :] * ext_ref[hl - pad + j:hl - pad + j + ts, :]
    xc = (acc * _sigmoid(acc)).astype(BF16)
    xc_ref[0] = xc
    gw = PLE_GROUP
    gates = jnp.broadcast_to(bg_ref[...], (ts, GATE_LANES))
    for j in range(c // gw):
        cs = slice(j * gw, (j + 1) * gw)
        qk = _dot(xc[:, cs], wqk_ref[j])
        qj = qk[:, :gw].astype(BF16)
        kj = qk[:, gw:]
        kjb = kj.astype(BF16)
        vj = _dot(xm[:, cs], wv_ref[j]).astype(BF16)
        q_ref[0, :, cs] = qj
        k_ref[0, :, cs] = (kj * k_scale).astype(BF16)
        v_ref[0, :, cs] = vj
        gates = gates + _dot(qj, wg_ref[0, cs, :]) + _dot(kjb, wg_ref[1, cs, :]) + _dot(vj, wg_ref[2, cs, :])
    col = lax.broadcasted_iota(jnp.int32, gates.shape, 1)
    is_forget = (col % (2 * M_HEADS)) >= M_HEADS
    gates = jnp.where(is_forget, _log_sigmoid(gates), gates)
    g_ref[0] = gates
    gates_t = gates.T
    for ci in range(ts // SCAN_CHUNK):
        gt_ref[0, ci] = gates_t[:4 * M_HEADS, ci * SCAN_CHUNK:(ci + 1) * SCAN_CHUNK]


def _mlstm_in(x, norm_g, w_in, conv_w, conv_b, wqk, wv, wg, bg, k_scale):
    b, s, d = x.shape
    c = w_in.shape[1] // 2
    ts = TOKEN_TILE
    hb = ts // CONV_HALO
    nh_blocks = s // CONV_HALO
    tile = lambda w: pl.BlockSpec((1, ts, w), lambda bi, i: (bi, i, 0))
    return pl.pallas_call(
        functools.partial(_min_body, k_scale),
        grid=(b, s // ts),
        in_specs=[
            tile(d),
            pl.BlockSpec((1, CONV_HALO, d), lambda bi, i: (bi, jnp.maximum(i * hb - 1, 0), 0)),
            pl.BlockSpec((1, CONV_HALO, d), lambda bi, i: (bi, jnp.minimum((i + 1) * hb, nh_blocks - 1), 0)),
            _const_spec((1, d)), _const_spec(w_in.shape), _const_spec(conv_w.shape), _const_spec((1, c)),
            _const_spec(wqk.shape), _const_spec(wv.shape), _const_spec(wg.shape), _const_spec(bg.shape),
        ],
        out_specs=[tile(c)] * 5 + [tile(GATE_LANES),
                                   pl.BlockSpec((1, ts // SCAN_CHUNK, 4 * M_HEADS, SCAN_CHUNK),
                                                lambda bi, i: (bi, i, 0, 0))],
        out_shape=[jax.ShapeDtypeStruct((b, s, c), BF16)] * 5
        + [jax.ShapeDtypeStruct((b, s, GATE_LANES), F32),
           jax.ShapeDtypeStruct((b, s // SCAN_CHUNK, 4 * M_HEADS, SCAN_CHUNK), F32)],
        scratch_shapes=[pltpu.VMEM((ts + 2 * CONV_HALO, c), F32)],
        compiler_params=_params(2),
        name="mlstm_in",
    )(x, x, x, norm_g.reshape(1, d), w_in, conv_w, conv_b.reshape(1, c), wqk, wv, wg, bg)


def _running(x, axis, rev, op, fill):
    n = x.shape[axis]
    idx = lax.broadcasted_iota(jnp.int32, x.shape, axis)
    s = 1
    while s < n:
        if rev:
            x = op(x, jnp.where(idx < n - s, pltpu.roll(x, n - s, axis), fill))
        else:
            x = op(x, jnp.where(idx >= s, pltpu.roll(x, s, axis), fill))
        s *= 2
    return x


def _zero_rows_at_first_chunk(refs, rows_per_trip):
    rows = refs[0].shape[0]

    def zero(i, carry):
        r0 = pl.multiple_of(i * rows_per_trip, rows_per_trip)
        for ref in refs:
            ref[pl.ds(r0, rows_per_trip), :] = jnp.zeros((rows_per_trip, ref.shape[1]), F32)
        return carry

    lax.fori_loop(0, jnp.where(pl.program_id(1) == 0, rows // rows_per_trip, 0), zero, 0)


def _mlstm_body(rev, q_ref, k_ref, v_ref, g_ref, gt_ref, o_ref, *state_refs):
    nh = M_HEADS
    c_refs, n_refs, m_refs = state_refs[:nh], state_refs[nh:2 * nh], state_refs[2 * nh:]
    ln = SCAN_CHUNK
    n_chunks = q_ref.shape[1] // ln
    dh = q_ref.shape[2] // nh
    _zero_rows_at_first_chunk(c_refs, 64)
    _zero_rows_at_first_chunk(n_refs + m_refs, 8)

    tt = lax.broadcasted_iota(jnp.int32, (ln, ln), 0)
    ss = lax.broadcasted_iota(jnp.int32, (ln, ln), 1)
    valid = (ss >= tt) if rev else (ss <= tt)
    end = 0 if rev else ln - 1
    goff = 2 * nh if rev else 0

    def chunk(i, carry):
        ck = (n_chunks - 1 - i) if rev else i
        rows = pl.ds(pl.multiple_of(ck * ln, ln), ln)
        g = g_ref[0, rows, :] * LOG2_E
        gt = gt_ref[0, ck] * LOG2_E
        bcol = _running(g, 0, rev, jnp.add, 0.0)
        brow = _running(gt, 1, rev, jnp.add, 0.0)
        gdcol = g - pltpu.roll(bcol, g.shape[1] - nh, 1)
        gdrow = gt - pltpu.roll(brow, gt.shape[0] - nh, 0)
        mcol = _running(gdcol, 0, rev, jnp.maximum, -jnp.inf)
        for h in range(nh):
            hs = slice(h * dh, (h + 1) * dh)
            ci, cf = goff + h, goff + nh + h
            gd_row, gd_col = gdrow[ci:ci + 1, :], gdcol[:, ci:ci + 1]
            b_col = bcol[:, cf:cf + 1]
            b_all = b_col[end:end + 1, :]
            m_prev = m_refs[h][0:1, 0:1]
            big_m = jnp.maximum(mcol[:, ci:ci + 1], m_prev)
            m_last = big_m[end:end + 1, :]
            q, k, v = q_ref[0, rows, hs], k_ref[0, rows, hs], v_ref[0, rows, hs]

            s_mat = _dot_nt(q, k) * jnp.exp2(jnp.where(valid, gd_row - big_m, -jnp.inf))
            w_inter = jnp.exp2(m_prev - big_m)
            cst = c_refs[h][...]
            nst = n_refs[h][0:1, :]
            num = _dot(s_mat.astype(BF16), v) + w_inter * _dot(q, cst.astype(BF16))
            den = jnp.sum(s_mat, axis=-1, keepdims=True) + \
                w_inter * jnp.sum(q.astype(F32) * nst, axis=-1, keepdims=True)
            hh = num * (1.0 / jnp.maximum(jnp.abs(den), jnp.exp2(-(b_col + big_m))))
            o_ref[0, rows, hs] = hh.astype(o_ref.dtype)

            wk_col = jnp.exp2(gd_col - m_last)
            w_old = jnp.exp2(m_prev - m_last)
            kw = k.astype(F32) * wk_col
            c_refs[h][...] = w_old * cst + _dot_tn(kw.astype(BF16), v)
            n_refs[h][...] = jnp.broadcast_to(w_old * nst + jnp.sum(kw, axis=0, keepdims=True), n_refs[h].shape)
            m_refs[h][...] = jnp.broadcast_to(b_all + m_last, m_refs[h].shape)
        return carry

    lax.fori_loop(0, n_chunks, chunk, 0)


def _mlstm_scan(rev, q, k, v, g, gt):
    b, s, c = q.shape
    ln = SCAN_CHUNK * SCAN_STEP_CHUNKS
    nc = s // ln
    dh = c // M_HEADS
    cidx = (lambda i: nc - 1 - i) if rev else (lambda i: i)
    tile = lambda w: pl.BlockSpec((1, ln, w), lambda bi, i: (bi, cidx(i), 0))
    return pl.pallas_call(
        functools.partial(_mlstm_body, rev),
        grid=(b, nc),
        in_specs=[tile(c), tile(c), tile(c), tile(GATE_LANES),
                  pl.BlockSpec((1, SCAN_STEP_CHUNKS, 4 * M_HEADS, SCAN_CHUNK), lambda bi, i: (bi, cidx(i), 0, 0))],
        out_specs=tile(c),
        out_shape=jax.ShapeDtypeStruct((b, s, c), BF16),
        scratch_shapes=[pltpu.VMEM((dh, dh), F32)] * M_HEADS + [pltpu.VMEM((8, dh), F32)] * M_HEADS
        + [pltpu.VMEM((8, 128), F32)] * M_HEADS,
        compiler_params=_params(2),
        name="mlstm_scan_bwd" if rev else "mlstm_scan_fwd",
    )(q, k, v, g, gt)


def _blockdiag_tiles(w):
    nb, bc, bd = w.shape
    per = PLE_GROUP // bc
    w = w.reshape(nb // per, per, bc, bd)
    eye = jnp.eye(per, dtype=w.dtype)
    dense = jnp.einsum('jncd,nm->jncmd', w, eye)
    return dense.reshape(nb // per, per * bc, per * bd)


def _mlstm_mixer(x, mix_g, w_in, conv_w, conv_b, wq, wk, wv, w_gate, b_gate, head_norm, skip, w_out):
    b, s, d = x.shape
    c = w_out.shape[0]
    wqk =jnp.concatenate([_blockdiag_tiles(wq), _blockdiag_tiles(wk)], axis=-1).astype(BF16)
    wvt = _blockdiag_tiles(wv).astype(BF16)
    ng = w_gate.shape[1]
    wg = jnp.pad(w_gate.reshape(3, c, ng), ((0, 0), (0, 0), (0, GATE_LANES - ng))).astype(BF16)
    bg = jnp.pad(b_gate.reshape(1, ng), ((0, 0), (0, GATE_LANES - ng)))
    k_scale = float(c // M_HEADS) ** -0.5
    q, k, v, xc, z, g, gt = _mlstm_in(x, mix_g, w_in.astype(BF16), conv_w, conv_b, wqk, wvt, wg, bg, k_scale)
    flat = lambda t: t.reshape(b * s, c)
    h_f = _mlstm_scan(False, q, k, v, g, gt)
    h_b = _mlstm_scan(True, q, k, v, g, gt)
    return ("mlstm", (flat(h_f), flat(h_b), head_norm, flat(xc), skip, flat(z), w_out.astype(BF16)),
            (True, True, False, True, False, True, False))


def _hpre_body(x_ref, g_ref, w_ref, bf_ref, lb_ref, q_ref, kf_ref, kb_ref, v_ref, lff_ref, lfb_ref, gg_ref):
    d = x_ref.shape[-1]
    xn = _rms(x_ref[...], g_ref[...]).astype(BF16)
    proj = lambda col: _dot(xn, w_ref[:, col * d:(col + 1) * d])
    q_ref[...] = proj(0).astype(BF16)
    v_ref[...] = proj(3).astype(BF16)
    gg_ref[...] = proj(4).astype(BF16)
    for di, (k_ref, lf_ref) in enumerate(((kf_ref, lff_ref), (kb_ref, lfb_ref))):
        a = proj(1 + di) + bf_ref[di:di + 1, :]
        lb = lb_ref[di:di + 1, :]
        e = jnp.exp(-jnp.abs(a))
        inv = 1.0 / (1.0 + e)
        pos = a >= 0.0
        sig = jnp.where(pos, inv, e * inv)
        nsig = jnp.where(pos, e * inv, inv)
        lf_ref[...] = jnp.log(lb + (1.0 - lb) * sig) * LOG2_E
        k_ref[...] = ((1.0 - lb) * nsig).astype(BF16)


def _hpre(x, g, w_in, b_f, lb):
    t, d = x.shape
    tm = TOKEN_TILE
    row = pl.BlockSpec((tm, d), lambda i: (i, 0))
    shape = lambda dt: jax.ShapeDtypeStruct((t, d), dt)
    return pl.pallas_call(
        _hpre_body,
        grid=(t // tm,),
        in_specs=[row, _const_spec((1, d)), _const_spec(w_in.shape), _const_spec((2, d)), _const_spec((2, d))],
        out_specs=[row] * 7,
        out_shape=[shape(BF16)] * 4 + [shape(F32)] * 2 + [shape(BF16)],
        compiler_params=_params(1),
        name="hgrn_pre",
    )(x, g.reshape(1, d), w_in, b_f, lb)


def _hgrn_body(rev, q_ref, k_ref, v_ref, lf_ref, o_ref, *st_refs):
    nh = len(st_refs)
    ln = SCAN_CHUNK
    n_chunks = q_ref.shape[1] // ln
    dk = q_ref.shape[2] // nh
    hb = ln // 2
    sub = 8
    ng = ln // sub
    qside = 0 if rev else 1
    _zero_rows_at_first_chunk(st_refs, 32)

    r8 = lax.broadcasted_iota(jnp.int32, (ng, sub, dk), 1)
    tt = lax.broadcasted_iota(jnp.int32, (hb, hb), 0)
    ss = lax.broadcasted_iota(jnp.int32, (hb, hb), 1)
    lvl = (lax.bitcast_convert_type((tt ^ ss).astype(F32), jnp.int32) >> 23) - 127
    lvl = jnp.where((ss > tt) if rev else (ss < tt), lvl, jnp.where(tt == ss, -1, -2))
    halves = (slice(0, hb), slice(hb, ln))
    bcast = lambda t, i: jnp.broadcast_to(t[:, i:i + 1, :], t.shape)

    def head(h, rows):
        hs = slice(h * dk, (h + 1) * dk)
        lf = lf_ref[0, rows, hs]
        q, k, v = q_ref[0, rows, hs], k_ref[0, rows, hs], v_ref[0, rows, hs]
        qf, kf = q.astype(F32), k.astype(F32)
        z_up, z_lo = (kf, qf) if rev else (qf, kf)
        a = [jnp.where(lvl == -1, _dot_nt(q[hv], k[hv]), 0.0) for hv in halves]

        lf3 = lf.reshape(ng, sub, dk)
        z3_up, z3_lo = z_up.reshape(ng, sub, dk), z_lo.reshape(ng, sub, dk)
        cc = lf3
        for s in (1, 2, 4):
            if rev:
                cc = cc + jnp.where(r8 < sub - s, pltpu.roll(cc, sub - s, 1), 0.0)
            else:
                cc = cc + jnp.where(r8 >= s, pltpu.roll(cc, s, 1), 0.0)
        for j, m in enumerate((1, 2, 4)):
            upper = (r8 & m) != 0
            ref = m if rev else m - 1
            if m == 1:
                e = jnp.where(upper != rev, lf3, 0.0)
            elif m == 2:
                e = -jnp.abs(cc - jnp.where(r8 < 4, bcast(cc, ref), bcast(cc, ref + 4)))
            else:
                e = -jnp.abs(cc - bcast(cc, ref))
            w = (jnp.where(upper, z3_up, z3_lo) * jnp.exp2(e)).reshape(ln, dk).astype(BF16)
            for i, hv in enumerate(halves):
                a[i] = jnp.where(lvl == j, _dot_nt(w[hv], w[hv]), a[i])
        if rev:
            y = cc
            tot = bcast(cc, 0)
            x = tot - cc
        else:
            x = cc
            tot = bcast(cc, sub - 1)
            y = tot - cc
        x, y, tot = x.reshape(ln, dk), y.reshape(ln, dk), tot.reshape(ln, dk)

        m, j = sub, 3
        while m < ln:
            nb = ln // (2 * m)
            split = lambda t: t.reshape(nb, 2, m, t.shape[-1])
            join = lambda lo, up: jnp.stack([lo, up], axis=1).reshape(ln, dk)
            x4, y4, t4 = split(x), split(y), split(tot)
            w_lo = split(z_lo)[:, 0] * jnp.exp2(y4[:, 0])
            w_up = split(z_up)[:, 1] * jnp.exp2(x4[:, 1])
            if m < hb:
                w = join(w_lo, w_up).astype(BF16)
                wq = (w_lo if rev else w_up).astype(BF16)
                nbh = nb // 2
                mask = lvl.reshape(nbh, 2, m, hb)[:, qside] == j
                for i, hv in enumerate(halves):
                    p = _dot_nt(wq[i * nbh:(i + 1) * nbh].reshape(nbh * m, dk), w[hv])
                    a4 = a[i].reshape(nbh, 2, m, hb)
                    aq = jnp.where(mask, p.reshape(nbh, m, hb), a4[:, qside])
                    parts = [aq, a4[:, 1]] if rev else [a4[:, 0], aq]
                    a[i] = jnp.stack(parts, axis=1).reshape(hb, hb)
            else:
                wl, wu = w_lo.reshape(hb, dk).astype(BF16), w_up.reshape(hb, dk).astype(BF16)
                cross = _dot_nt(wl, wu) if rev else _dot_nt(wu, wl)
            x = join(x4[:, 0], x4[:, 1] + t4[:, 0])
            y = join(y4[:, 0] + t4[:, 1], y4[:, 1])
            tsum = t4[:, 0] + t4[:, 1]
            tot = join(tsum, tsum)
            m, j = 2 * m, j + 1
        e_q, e_k = (y, x) if rev else (x, y)
        st = st_refs[h][...]
        inter = _dot_nt((qf * jnp.exp2(e_q)).astype(BF16), st.astype(BF16))
        a0, a1, cr = a[0].astype(BF16), a[1].astype(BF16), cross.astype(BF16)
        if rev:
            o0 = _dot(jnp.concatenate([a0, cr], axis=1), v)
            o1 = _dot(a1, v[halves[1]])
        else:
            o0 = _dot(a0, v[halves[0]])
            o1 = _dot(jnp.concatenate([cr, a1], axis=1), v)
        out = jnp.concatenate([o0 + inter[halves[0]], o1 + inter[halves[1]]], axis=0)
        o_ref[0, rows, hs] = out.astype(o_ref.dtype)
        kw = (kf * jnp.exp2(e_k)).astype(BF16)
        st_refs[h][...] = jnp.exp2(tot[0:1, :]) * st + _dot_tn(v, kw)

    def chunk(i, carry):
        ck = (n_chunks - 1 - i) if rev else i
        rows = pl.ds(pl.multiple_of(ck * ln, ln), ln)
        for h in range(nh):
            head(h, rows)
        return carry

    lax.fori_loop(0, n_chunks, chunk, 0)


def _hgrn_scan(rev, q, k, v, lf):
    b, s, d = q.shape
    nh = H_HEADS
    dk = d // nh
    ln = SCAN_CHUNK * SCAN_STEP_CHUNKS
    nc = s // ln
    cidx = (lambda i: nc - 1 - i) if rev else (lambda i: i)
    tile = pl.BlockSpec((1, ln, d), lambda bi, i: (bi, cidx(i), 0))
    return pl.pallas_call(
        functools.partial(_hgrn_body, rev),
        grid=(b, nc),
        in_specs=[tile] * 4,
        out_specs=tile,
        out_shape=jax.ShapeDtypeStruct((b, s, d), BF16),
        scratch_shapes=[pltpu.VMEM((dk, dk), F32)] * nh,
        compiler_params=_params(2),
        name="hgrn_scan_bwd" if rev else "hgrn_scan_fwd",
    )(q, k, v, lf)


def _lower_bounds(logits):
    pr = jax.nn.softmax(logits.astype(F32), axis=0)
    return jnp.cumsum(pr, axis=0) - pr[0]


def _hgrn_mixer(x, mix_g, w_in, b_f, lb, head_norm, w_out):
    b, s, d = x.shape
    q, k_f, k_b, v, lf_f, lf_b, gg = _hpre(x.reshape(b * s, d), mix_g, w_in.astype(BF16), b_f, lb)
    seq = lambda t: t.reshape(b, s, d)
    o_f = _hgrn_scan(False, seq(q), seq(k_f), seq(v), seq(lf_f))
    o_b = _hgrn_scan(True, seq(q), seq(k_b), seq(v), seq(lf_b))
    flat = lambda t: t.reshape(b * s, d)
    return "hgrn", (flat(o_f), flat(o_b), head_norm, gg, w_out.astype(BF16)), (True, True, False, True, False)


def _trunk(x, p, ffn1_norm, ffn1_w_in, ffn1_w_out, mix_norm,
           m_w_in, m_conv_w, m_conv_b, m_wq, m_wk, m_wv, m_w_gate, m_b_gate, m_head_norm, m_skip, m_w_out,
           h_w_in, h_b_f, h_lb_logits, h_head_norm, h_w_out,
           ffn2_norm, ffn2_w_in, ffn2_w_out, ple_norm, ple_w_gate, ple_w_proj, final_norm):
    b, s, d = x.shape
    depth = ffn1_norm.shape[0]
    n_mixers = 2
    lb_fwd = _lower_bounds(h_lb_logits[0])
    lb_bwd = _lower_bounds(h_lb_logits[1])
    x = x.reshape(b * s, d)
    for i in range(depth):
        x = _ffn(x, ffn1_norm[i], ffn1_w_in[i].astype(BF16), ffn1_w_out[i].astype(BF16))
        j = i // n_mixers
        xs = x.reshape(b, s, d)
        if i % n_mixers == 0:
            mix = _mlstm_mixer(xs, mix_norm[i], m_w_in[j], m_conv_w[j], m_conv_b[j], m_wq[j], m_wk[j], m_wv[j],
                               m_w_gate[j], m_b_gate[j], m_head_norm[j], m_skip[j], m_w_out[j])
        else:
            lb = jnp.stack([lb_fwd[i], lb_bwd[i]])
            mix = _hgrn_mixer(xs, mix_norm[i], h_w_in[j], h_b_f[j], lb, h_head_norm[j], h_w_out[j])
        ple = (p.reshape(depth, b * s, -1), i, ple_norm[i], ple_w_gate[i].astype(BF16), ple_w_proj[i].astype(BF16))
        x = _ffn(x, ffn2_norm[i], ffn2_w_in[i].astype(BF16), ffn2_w_out[i].astype(BF16), mix=mix, ple=ple,
                 final_g=final_norm if i == depth - 1 else None)
    return x.reshape(b, s, d)


def kernel(x_prompt, x_sample, p_prompt, p_sample, ffn1_norm, ffn1_w_in, ffn1_w_out, mix_norm, m_w_in, m_conv_w, m_conv_b, m_wq, m_wk, m_wv, m_w_gate, m_b_gate, m_head_norm, m_skip, m_w_out, h_w_in, h_b_f, h_lb_logits, h_head_norm, h_w_out, ffn2_norm, ffn2_w_in, ffn2_w_out, ple_norm, ple_w_gate, ple_w_proj, final_norm):
    weights = (ffn1_norm, ffn1_w_in, ffn1_w_out, mix_norm,
               m_w_in, m_conv_w, m_conv_b, m_wq, m_wk, m_wv, m_w_gate, m_b_gate, m_head_norm, m_skip, m_w_out,
               h_w_in, h_b_f, h_lb_logits, h_head_norm, h_w_out,
               ffn2_norm, ffn2_w_in, ffn2_w_out, ple_norm, ple_w_gate, ple_w_proj, final_norm)
    return (_trunk(x_prompt, p_prompt, *weights), _trunk(x_sample, p_sample, *weights))
```

```python
import functools

import jax
import jax.numpy as jnp
from jax import lax
from jax.experimental import pallas as pl
from jax.experimental.pallas import tpu as pltpu

F32 = jnp.float32
BF16 = jnp.bfloat16
EPS = 1e-6
LOG2_E = 1.4426950408889634

M_HEADS = 4
M_QKV_BLOCK = 4
M_CONV = 5
H_HEADS = 8
PLE_GROUP = 256
TOKEN_TILE = 512
SCAN_CHUNK = 256
SCAN_STEP_CHUNKS = 4
CONV_HALO = 8
GATE_LANES = 128
VMEM_LIMIT = 56 * 1024 * 1024
VMEM_LIMIT_MIX_OUT = 62 * 1024 * 1024


def _params(n_axes, vmem_limit=VMEM_LIMIT):
    return pltpu.CompilerParams(dimension_semantics=("arbitrary",) * n_axes,
                                vmem_limit_bytes=vmem_limit)


def _const_spec(shape):
    nd = len(shape)
    return pl.BlockSpec(shape, lambda *_: (0,) * nd, pipeline_mode=pl.Buffered(1))


def _rms(x, g):
    return x * lax.rsqrt(jnp.mean(x * x, axis=-1, keepdims=True) + EPS) * g


def _sigmoid(x):
    return 0.5 * jnp.tanh(0.5 * x) + 0.5


def _dot(a, b):
    return jnp.dot(a, b, preferred_element_type=F32)


def _dot_nt(a, b):
    return lax.dot_general(a, b, (((1,), (1,)), ((), ())), preferred_element_type=F32)


def _dot_tn(a, b):
    return lax.dot_general(a, b, (((0,), (0,)), ((), ())), preferred_element_type=F32)


def _mix_out(n_heads, ha_ref, hb_ref, hn_ref, add_ref, addw_ref, gate_ref, wo_ref, gate_fn):
    dh = ha_ref.shape[-1] // n_heads
    parts = []
    for h in range(n_heads):
        hs = slice(h * dh, (h + 1) * dh)
        hsum = ha_ref[:, hs].astype(F32) + hb_ref[:, hs].astype(F32)
        y = hsum * lax.rsqrt(jnp.mean(hsum * hsum, axis=-1, keepdims=True) + EPS) * hn_ref[:, hs]
        if add_ref is not None:
            y = y + addw_ref[:, hs] * add_ref[:, hs].astype(F32)
        parts.append((y * gate_fn(gate_ref[:, hs].astype(F32))).astype(BF16))
    return _dot(jnp.concatenate(parts, axis=-1), wo_ref[...])


def _ffn_body(mix, has_ple, has_final, *refs):
    it = iter(refs)
    x = next(it)[...]
    if mix == "mlstm":
        ha, hb, hn, xc, sk, z, wo = (next(it) for _ in range(7))
        x = x + _mix_out(M_HEADS, ha, hb, hn, xc, sk, z, wo, lambda t: t * _sigmoid(t))
    elif mix == "hgrn":
        ha, hb, hn, gg, wo = (next(it) for _ in range(5))
        x = x + _mix_out(H_HEADS, ha, hb, hn, None, None, gg, wo, _sigmoid)
    g_ref, win_ref, wout_ref = (next(it) for _ in range(3))
    if has_ple:
        p_ref, pg_ref, pwg_ref, pwp_ref = (next(it) for _ in range(4))
    if has_final:
        fg_ref = next(it)
    o_ref = next(it)
    d_ff = wout_ref.shape[0]
    xn = _rms(x, g_ref[...]).astype(BF16)
    a = _dot(xn, win_ref[:, :d_ff])
    u = _dot(xn, win_ref[:, d_ff:])
    act = (a * _sigmoid(a) * u).astype(BF16)
    x = x + 0.5 * _dot(act, wout_ref[...])
    if has_ple:
        xg = _rms(x, pg_ref[...]).astype(BF16)
        gate = _sigmoid(_dot(xg, pwg_ref[...]))
        x = x + gate * _dot(p_ref[...].astype(BF16), pwp_ref[...])
    if has_final:
        x = _rms(x, fg_ref[...])
    o_ref[...] = x


def _ffn(x, g, w_in, w_out, mix=None, ple=None, final_g=None):
    t, d = x.shape
    tm = TOKEN_TILE
    row = lambda w: pl.BlockSpec((tm, w), lambda i: (i, 0))
    args, specs = [x], [row(d)]
    kind = None
    if mix is not None:
        kind, operands, per_token = mix
        for item, tok in zip(operands, per_token):
            if item.ndim == 1:
                item = item.reshape(1, -1)
            args.append(item)
            specs.append(row(item.shape[1]) if tok else _const_spec(item.shape))
    args += [g.reshape(1, d), w_in, w_out]
    specs += [_const_spec((1, d)), _const_spec(w_in.shape), _const_spec(w_out.shape)]
    if ple is not None:
        p, layer, pg, pwg, pwp = ple
        args += [p, pg.reshape(1, d), pwg, pwp]
        specs += [pl.BlockSpec((None, tm, p.shape[2]), lambda i: (layer, i, 0)),
                  _const_spec((1, d)), _const_spec(pwg.shape), _const_spec(pwp.shape)]
    if final_g is not None:
        args.append(final_g.reshape(1, d))
        specs.append(_const_spec((1, d)))
    return pl.pallas_call(
        functools.partial(_ffn_body, kind, ple is not None, final_g is not None),
        grid=(t // tm,),
        in_specs=specs,
        out_specs=row(d),
        out_shape=jax.ShapeDtypeStruct((t, d), F32),
        compiler_params=_params(1, VMEM_LIMIT if mix is None else VMEM_LIMIT_MIX_OUT),
        name="ffn" if mix is None else "mix_out_ffn",
    )(*args)


def _log_sigmoid(x):
    return jnp.minimum(x, 0.0) - jnp.log1p(jnp.exp(-jnp.abs(x)))


def _min_body(k_scale, x_ref, xprev_ref, xnext_ref, ng_ref, win_ref, cw_ref, cb_ref, wqk_ref, wv_ref, wg_ref,
              bg_ref, q_ref, k_ref, v_ref, xc_ref, z_ref, g_ref, gt_ref, ext_ref):
    i = pl.program_id(1)
    last = pl.num_programs(1) - 1
    ts = x_ref.shape[1]
    c = q_ref.shape[2]
    hl = CONV_HALO
    pad = M_CONV // 2
    ng = ng_ref[...]
    xn = _rms(x_ref[0], ng)
    xn_ext = jnp.concatenate([_rms(xprev_ref[0], ng), xn, _rms(xnext_ref[0], ng)], axis=0).astype(BF16)
    xn = xn.astype(BF16)
    z_ref[0] = _dot(xn, win_ref[:, c:]).astype(BF16)
    xm_ext = _dot(xn_ext, win_ref[:, :c])
    r = lax.broadcasted_iota(jnp.int32, (ts + 2 * hl, 1), 0)
    inside = ((r >= hl) | (i > 0)) & ((r < hl + ts) | (i < last))
    ext_ref[...] = jnp.where(inside, xm_ext, 0.0)
    xm = ext_ref[hl:hl + ts, :].astype(BF16)
    acc = jnp.broadcast_to(cb_ref[...], (ts, c))
    for j in range(M_CONV):
        acc = acc + cw_ref[j:j + 1, :] * ext_ref[hl - pad + j:hl - pad + j + ts, :]
    xc = (acc * _sigmoid(acc)).astype(BF16)
    xc_ref[0] = xc
    gw = PLE_GROUP
    gates = jnp.broadcast_to(bg_ref[...], (ts, GATE_LANES))
    for j in range(c // gw):
        cs = slice(j * gw, (j + 1) * gw)
        qk = _dot(xc[:, cs], wqk_ref[j])
        qj = qk[:, :gw].astype(BF16)
        kj = qk[:, gw:]
        kjb = kj.astype(BF16)
        vj = _dot(xm[:, cs], wv_ref[j]).astype(BF16)
        q_ref[0, :, cs] = qj
        k_ref[0, :, cs] = (kj * k_scale).astype(BF16)
        v_ref[0, :, cs] = vj
        gates = gates + _dot(qj, wg_ref[0, cs, :]) + _dot(kjb, wg_ref[1, cs, :]) + _dot(vj, wg_ref[2, cs, :])
    col = lax.broadcasted_iota(jnp.int32, gates.shape, 1)
    is_forget = (col % (2 * M_HEADS)) >= M_HEADS
    gates = jnp.where(is_forget, _log_sigmoid(gates), gates)
    g_ref[0] = gates
    gates_t = gates.T
    for ci in range(ts // SCAN_CHUNK):
        gt_ref[0, ci] = gates_t[:4 * M_HEADS, ci * SCAN_CHUNK:(ci + 1) * SCAN_CHUNK]


def _mlstm_in(x, norm_g, w_in, conv_w, conv_b, wqk, wv, wg, bg, k_scale):
    b, s, d = x.shape
    c = w_in.shape[1] // 2
    ts = TOKEN_TILE
    hb = ts // CONV_HALO
    nh_blocks = s // CONV_HALO
    tile = lambda w: pl.BlockSpec((1, ts, w), lambda bi, i: (bi, i, 0))
    return pl.pallas_call(
        functools.partial(_min_body, k_scale),
        grid=(b, s // ts),
        in_specs=[
            tile(d),
            pl.BlockSpec((1, CONV_HALO, d), lambda bi, i: (bi, jnp.maximum(i * hb - 1, 0), 0)),
            pl.BlockSpec((1, CONV_HALO, d), lambda bi, i: (bi, jnp.minimum((i + 1) * hb, nh_blocks - 1), 0)),
            _const_spec((1, d)), _const_spec(w_in.shape), _const_spec(conv_w.shape), _const_spec((1, c)),
            _const_spec(wqk.shape), _const_spec(wv.shape), _const_spec(wg.shape), _const_spec(bg.shape),
        ],
        out_specs=[tile(c)] * 5 + [tile(GATE_LANES),
                                   pl.BlockSpec((1, ts // SCAN_CHUNK, 4 * M_HEADS, SCAN_CHUNK),
                                                lambda bi, i: (bi, i, 0, 0))],
        out_shape=[jax.ShapeDtypeStruct((b, s, c), BF16)] * 5
        + [jax.ShapeDtypeStruct((b, s, GATE_LANES), F32),
           jax.ShapeDtypeStruct((b, s // SCAN_CHUNK, 4 * M_HEADS, SCAN_CHUNK), F32)],
        scratch_shapes=[pltpu.VMEM((ts + 2 * CONV_HALO, c), F32)],
        compiler_params=_params(2),
        name="mlstm_in",
    )(x, x, x, norm_g.reshape(1, d), w_in, conv_w, conv_b.reshape(1, c), wqk, wv, wg, bg)


def _running(x, axis, rev, op, fill):
    n = x.shape[axis]
    idx = lax.broadcasted_iota(jnp.int32, x.shape, axis)
    s = 1
    while s < n:
        if rev:
            x = op(x, jnp.where(idx < n - s, pltpu.roll(x, n - s, axis), fill))
        else:
            x = op(x, jnp.where(idx >= s, pltpu.roll(x, s, axis), fill))
        s *= 2
    return x


def _zero_rows_at_first_chunk(refs, rows_per_trip):
    rows = refs[0].shape[0]

    def zero(i, carry):
        r0 = pl.multiple_of(i * rows_per_trip, rows_per_trip)
        for ref in refs:
            ref[pl.ds(r0, rows_per_trip), :] = jnp.zeros((rows_per_trip, ref.shape[1]), F32)
        return carry

    lax.fori_loop(0, jnp.where(pl.program_id(1) == 0, rows // rows_per_trip, 0), zero, 0)


def _mlstm_body(rev, q_ref, k_ref, v_ref, g_ref, gt_ref, o_ref, *state_refs):
    nh = M_HEADS
    c_refs, n_refs, m_refs = state_refs[:nh], state_refs[nh:2 * nh], state_refs[2 * nh:]
    ln = SCAN_CHUNK
    n_chunks = q_ref.shape[1] // ln
    dh = q_ref.shape[2] // nh
    _zero_rows_at_first_chunk(c_refs, 64)
    _zero_rows_at_first_chunk(n_refs + m_refs, 8)

    tt = lax.broadcasted_iota(jnp.int32, (ln, ln), 0)
    ss = lax.broadcasted_iota(jnp.int32, (ln, ln), 1)
    valid = (ss >= tt) if rev else (ss <= tt)
    end = 0 if rev else ln - 1
    goff = 2 * nh if rev else 0

    def chunk(i, carry):
        ck = (n_chunks - 1 - i) if rev else i
        rows = pl.ds(pl.multiple_of(ck * ln, ln), ln)
        g = g_ref[0, rows, :] * LOG2_E
        gt = gt_ref[0, ck] * LOG2_E
        bcol = _running(g, 0, rev, jnp.add, 0.0)
        brow = _running(gt, 1, rev, jnp.add, 0.0)
        gdcol = g - pltpu.roll(bcol, g.shape[1] - nh, 1)
        gdrow = gt - pltpu.roll(brow, gt.shape[0] - nh, 0)
        mcol = _running(gdcol, 0, rev, jnp.maximum, -jnp.inf)
        for h in range(nh):
            hs = slice(h * dh, (h + 1) * dh)
            ci, cf = goff + h, goff + nh + h
            gd_row, gd_col = gdrow[ci:ci + 1, :], gdcol[:, ci:ci + 1]
            b_col = bcol[:, cf:cf + 1]
            b_all = b_col[end:end + 1, :]
            m_prev = m_refs[h][0:1, 0:1]
            big_m = jnp.maximum(mcol[:, ci:ci + 1], m_prev)
            m_last = big_m[end:end + 1, :]
            q, k, v = q_ref[0, rows, hs], k_ref[0, rows, hs], v_ref[0, rows, hs]

            s_mat = _dot_nt(q, k) * jnp.exp2(jnp.where(valid, gd_row - big_m, -jnp.inf))
            w_inter = jnp.exp2(m_prev - big_m)
            cst = c_refs[h][...]
            nst = n_refs[h][0:1, :]
            den = jnp.sum(s_mat, axis=-1, keepdims=True) + \
                w_inter * jnp.sum(q.astype(F32) * nst, axis=-1, keepdims=True)
            inv = 1.0 / jnp.maximum(jnp.abs(den), jnp.exp2(-(b_col + big_m)))
            s_bf = s_mat.astype(BF16)
            hw = dh // 2
            for c0 in (0, hw):
                num = _dot(s_bf, v[:, c0:c0 + hw]) + w_inter * _dot(q, cst[:, c0:c0 + hw].astype(BF16))
                o_ref[0, rows, h * dh + c0:h * dh + c0 + hw] = (num * inv).astype(o_ref.dtype)

            wk_col = jnp.exp2(gd_col - m_last)
            w_old = jnp.exp2(m_prev - m_last)
            kw = k.astype(F32) * wk_col
            c_refs[h][...] = w_old * cst + _dot_tn(kw.astype(BF16), v)
            n_refs[h][...] = jnp.broadcast_to(w_old * nst + jnp.sum(kw, axis=0, keepdims=True), n_refs[h].shape)
            m_refs[h][...] = jnp.broadcast_to(b_all + m_last, m_refs[h].shape)
        return carry

    lax.fori_loop(0, n_chunks, chunk, 0)


def _mlstm_scan(rev, q, k, v, g, gt):
    b, s, c = q.shape
    ln = SCAN_CHUNK * SCAN_STEP_CHUNKS
    nc = s // ln
    dh = c // M_HEADS
    cidx = (lambda i: nc - 1 - i) if rev else (lambda i: i)
    tile = lambda w: pl.BlockSpec((1, ln, w), lambda bi, i: (bi, cidx(i), 0))
    return pl.pallas_call(
        functools.partial(_mlstm_body, rev),
        grid=(b, nc),
        in_specs=[tile(c), tile(c), tile(c), tile(GATE_LANES),
                  pl.BlockSpec((1, SCAN_STEP_CHUNKS, 4 * M_HEADS, SCAN_CHUNK), lambda bi, i: (bi, cidx(i), 0, 0))],
        out_specs=tile(c),
        out_shape=jax.ShapeDtypeStruct((b, s, c), BF16),
        scratch_shapes=[pltpu.VMEM((dh, dh), F32)] * M_HEADS + [pltpu.VMEM((8, dh), F32)] * M_HEADS
        + [pltpu.VMEM((8, 128), F32)] * M_HEADS,
        compiler_params=_params(2),
        name="mlstm_scan_bwd" if rev else "mlstm_scan_fwd",
    )(q, k, v, g, gt)


def _blockdiag_tiles(w):
    nb, bc, bd = w.shape
    per = PLE_GROUP // bc
    w = w.reshape(nb // per, per, bc, bd)
    eye = jnp.eye(per, dtype=w.dtype)
    dense = jnp.einsum('jncd,nm->jncmd', w, eye)
    return dense.reshape(nb // per, per * bc, per * bd)


def _mlstm_mixer(x, mix_g, w_in, conv_w, conv_b, wq, wk, wv, w_gate, b_gate, head_norm, skip, w_out):
    b, s, d = x.shape
    c = w_out.shape[0]
    wqk =jnp.concatenate([_blockdiag_tiles(wq), _blockdiag_tiles(wk)], axis=-1).astype(BF16)
    wvt = _blockdiag_tiles(wv).astype(BF16)
    ng = w_gate.shape[1]
    wg = jnp.pad(w_gate.reshape(3, c, ng), ((0, 0), (0, 0), (0, GATE_LANES - ng))).astype(BF16)
    bg = jnp.pad(b_gate.reshape(1, ng), ((0, 0), (0, GATE_LANES - ng)))
    k_scale = float(c // M_HEADS) ** -0.5
    q, k, v, xc, z, g, gt = _mlstm_in(x, mix_g, w_in.astype(BF16), conv_w, conv_b, wqk, wvt, wg, bg, k_scale)
    flat = lambda t: t.reshape(b * s, c)
    h_f = _mlstm_scan(False, q, k, v, g, gt)
    h_b = _mlstm_scan(True, q, k, v, g, gt)
    return ("mlstm", (flat(h_f), flat(h_b), head_norm, flat(xc), skip, flat(z), w_out.astype(BF16)),
            (True, True, False, True, False, True, False))


def _hpre_body(x_ref, g_ref, w_ref, bf_ref, lb_ref, q_ref, kf_ref, kb_ref, v_ref, lff_ref, lfb_ref, gg_ref):
    d = x_ref.shape[-1]
    xn = _rms(x_ref[...], g_ref[...]).astype(BF16)
    proj = lambda col: _dot(xn, w_ref[:, col * d:(col + 1) * d])
    q_ref[...] = proj(0).astype(BF16)
    v_ref[...] = proj(3).astype(BF16)
    gg_ref[...] = proj(4).astype(BF16)
    for di, (k_ref, lf_ref) in enumerate(((kf_ref, lff_ref), (kb_ref, lfb_ref))):
        a = proj(1 + di) + bf_ref[di:di + 1, :]
        lb = lb_ref[di:di + 1, :]
        e = jnp.exp(-jnp.abs(a))
        inv = 1.0 / (1.0 + e)
        pos = a >= 0.0
        sig = jnp.where(pos, inv, e * inv)
        nsig = jnp.where(pos, e * inv, inv)
        lf_ref[...] = jnp.log(lb + (1.0 - lb) * sig) * LOG2_E
        k_ref[...] = ((1.0 - lb) * nsig).astype(BF16)


def _hpre(x, g, w_in, b_f, lb):
    t, d = x.shape
    tm = TOKEN_TILE
    row = pl.BlockSpec((tm, d), lambda i: (i, 0))
    shape = lambda dt: jax.ShapeDtypeStruct((t, d), dt)
    return pl.pallas_call(
        _hpre_body,
        grid=(t // tm,),
        in_specs=[row, _const_spec((1, d)), _const_spec(w_in.shape), _const_spec((2, d)), _const_spec((2, d))],
        out_specs=[row] * 7,
        out_shape=[shape(BF16)] * 4 + [shape(F32)] * 2 + [shape(BF16)],
        compiler_params=_params(1),
        name="hgrn_pre",
    )(x, g.reshape(1, d), w_in, b_f, lb)


def _hgrn_body(rev, q_ref, k_ref, v_ref, lf_ref, o_ref, *st_refs):
    nh = len(st_refs)
    ln = SCAN_CHUNK
    n_chunks = q_ref.shape[1] // ln
    dk = q_ref.shape[2] // nh
    hb = ln // 2
    sub = 8
    ng = ln // sub
    qside = 0 if rev else 1
    _zero_rows_at_first_chunk(st_refs, 32)

    r8 = lax.broadcasted_iota(jnp.int32, (ng, sub, dk), 1)
    tt = lax.broadcasted_iota(jnp.int32, (hb, hb), 0)
    ss = lax.broadcasted_iota(jnp.int32, (hb, hb), 1)
    lvl = (lax.bitcast_convert_type((tt ^ ss).astype(F32), jnp.int32) >> 23) - 127
    lvl = jnp.where((ss > tt) if rev else (ss < tt), lvl, jnp.where(tt == ss, -1, -2))
    halves = (slice(0, hb), slice(hb, ln))
    bcast = lambda t, i: jnp.broadcast_to(t[:, i:i + 1, :], t.shape)

    def head(h, rows):
        hs = slice(h * dk, (h + 1) * dk)
        lf = lf_ref[0, rows, hs]
        q, k, v = q_ref[0, rows, hs], k_ref[0, rows, hs], v_ref[0, rows, hs]
        qf, kf = q.astype(F32), k.astype(F32)
        z_up, z_lo = (kf, qf) if rev else (qf, kf)
        a = [jnp.where(lvl == -1, _dot_nt(q[hv], k[hv]), 0.0) for hv in halves]

        lf3 = lf.reshape(ng, sub, dk)
        z3_up, z3_lo = z_up.reshape(ng, sub, dk), z_lo.reshape(ng, sub, dk)
        cc = lf3
        for s in (1, 2, 4):
            if rev:
                cc = cc + jnp.where(r8 < sub - s, pltpu.roll(cc, sub - s, 1), 0.0)
            else:
                cc = cc + jnp.where(r8 >= s, pltpu.roll(cc, s, 1), 0.0)
        for j, m in enumerate((1, 2, 4)):
            upper = (r8 & m) != 0
            ref = m if rev else m - 1
            if m == 1:
                e = jnp.where(upper != rev, lf3, 0.0)
            elif m == 2:
                e = -jnp.abs(cc - jnp.where(r8 < 4, bcast(cc, ref), bcast(cc, ref + 4)))
            else:
                e = -jnp.abs(cc - bcast(cc, ref))
            w = (jnp.where(upper, z3_up, z3_lo) * jnp.exp2(e)).reshape(ln, dk).astype(BF16)
            for i, hv in enumerate(halves):
                a[i] = jnp.where(lvl == j, _dot_nt(w[hv], w[hv]), a[i])
        if rev:
            y = cc
            tot = bcast(cc, 0)
            x = tot - cc
        else:
            x = cc
            tot = bcast(cc, sub - 1)
            y = tot - cc
        x, y, tot = x.reshape(ln, dk), y.reshape(ln, dk), tot.reshape(ln, dk)

        m, j = sub, 3
        while m < ln:
            nb = ln // (2 * m)
            split = lambda t: t.reshape(nb, 2, m, t.shape[-1])
            join = lambda lo, up: jnp.stack([lo, up], axis=1).reshape(ln, dk)
            x4, y4, t4 = split(x), split(y), split(tot)
            w_lo = split(z_lo)[:, 0] * jnp.exp2(y4[:, 0])
            w_up = split(z_up)[:, 1] * jnp.exp2(x4[:, 1])
            if m < hb:
                w = join(w_lo, w_up).astype(BF16)
                wq = (w_lo if rev else w_up).astype(BF16)
                nbh = nb // 2
                mask = lvl.reshape(nbh, 2, m, hb)[:, qside] == j
                for i, hv in enumerate(halves):
                    p = _dot_nt(wq[i * nbh:(i + 1) * nbh].reshape(nbh * m, dk), w[hv])
                    a4 = a[i].reshape(nbh, 2, m, hb)
                    aq = jnp.where(mask, p.reshape(nbh, m, hb), a4[:, qside])
                    parts = [aq, a4[:, 1]] if rev else [a4[:, 0], aq]
                    a[i] = jnp.stack(parts, axis=1).reshape(hb, hb)
            else:
                wl, wu = w_lo.reshape(hb, dk).astype(BF16), w_up.reshape(hb, dk).astype(BF16)
                cross = _dot_nt(wl, wu) if rev else _dot_nt(wu, wl)
            x = join(x4[:, 0], x4[:, 1] + t4[:, 0])
            y = join(y4[:, 0] + t4[:, 1], y4[:, 1])
            tsum = t4[:, 0] + t4[:, 1]
            tot = join(tsum, tsum)
            m, j = 2 * m, j + 1
        e_q, e_k = (y, x) if rev else (x, y)
        st = st_refs[h][...]
        inter = _dot_nt((qf * jnp.exp2(e_q)).astype(BF16), st.astype(BF16))
        a0, a1, cr = a[0].astype(BF16), a[1].astype(BF16), cross.astype(BF16)
        if rev:
            o0 = _dot(jnp.concatenate([a0, cr], axis=1), v)
            o1 = _dot(a1, v[halves[1]])
        else:
            o0 = _dot(a0, v[halves[0]])
            o1 = _dot(jnp.concatenate([cr, a1], axis=1), v)
        out = jnp.concatenate([o0 + inter[halves[0]], o1 + inter[halves[1]]], axis=0)
        o_ref[0, rows, hs] = out.astype(o_ref.dtype)
        kw = (kf * jnp.exp2(e_k)).astype(BF16)
        st_refs[h][...] = jnp.exp2(tot[0:1, :]) * st + _dot_tn(v, kw)

    def chunk(i, carry):
        ck = (n_chunks - 1 - i) if rev else i
        rows = pl.ds(pl.multiple_of(ck * ln, ln), ln)
        for h in range(nh):
            head(h, rows)
        return carry

    lax.fori_loop(0, n_chunks, chunk, 0)


def _hgrn_scan(rev, q, k, v, lf):
    b, s, d = q.shape
    nh = H_HEADS
    dk = d // nh
    ln = SCAN_CHUNK * SCAN_STEP_CHUNKS
    nc = s // ln
    cidx = (lambda i: nc - 1 - i) if rev else (lambda i: i)
    tile = pl.BlockSpec((1, ln, d), lambda bi, i: (bi, cidx(i), 0))
    return pl.pallas_call(
        functools.partial(_hgrn_body, rev),
        grid=(b, nc),
        in_specs=[tile] * 4,
        out_specs=tile,
        out_shape=jax.ShapeDtypeStruct((b, s, d), BF16),
        scratch_shapes=[pltpu.VMEM((dk, dk), F32)] * nh,
        compiler_params=_params(2),
        name="hgrn_scan_bwd" if rev else "hgrn_scan_fwd",
    )(q, k, v, lf)


def _lower_bounds(logits):
    pr = jax.nn.softmax(logits.astype(F32), axis=0)
    return jnp.cumsum(pr, axis=0) - pr[0]


def _hgrn_mixer(x, mix_g, w_in, b_f, lb, head_norm, w_out):
    b, s, d = x.shape
    q, k_f, k_b, v, lf_f, lf_b, gg = _hpre(x.reshape(b * s, d), mix_g, w_in.astype(BF16), b_f, lb)
    seq = lambda t: t.reshape(b, s, d)
    o_f = _hgrn_scan(False, seq(q), seq(k_f), seq(v), seq(lf_f))
    o_b = _hgrn_scan(True, seq(q), seq(k_b), seq(v), seq(lf_b))
    flat = lambda t: t.reshape(b * s, d)
    return "hgrn", (flat(o_f), flat(o_b), head_norm, gg, w_out.astype(BF16)), (True, True, False, True, False)


def _trunk(x, p, ffn1_norm, ffn1_w_in, ffn1_w_out, mix_norm,
           m_w_in, m_conv_w, m_conv_b, m_wq, m_wk, m_wv, m_w_gate, m_b_gate, m_head_norm, m_skip, m_w_out,
           h_w_in, h_b_f, h_lb_logits, h_head_norm, h_w_out,
           ffn2_norm, ffn2_w_in, ffn2_w_out, ple_norm, ple_w_gate, ple_w_proj, final_norm):
    b, s, d = x.shape
    depth = ffn1_norm.shape[0]
    n_mixers = 2
    lb_fwd = _lower_bounds(h_lb_logits[0])
    lb_bwd = _lower_bounds(h_lb_logits[1])
    x = x.reshape(b * s, d)
    for i in range(depth):
        x = _ffn(x, ffn1_norm[i], ffn1_w_in[i].astype(BF16), ffn1_w_out[i].astype(BF16))
        j = i // n_mixers
        xs = x.reshape(b, s, d)
        if i % n_mixers == 0:
            mix = _mlstm_mixer(xs, mix_norm[i], m_w_in[j], m_conv_w[j], m_conv_b[j], m_wq[j], m_wk[j], m_wv[j],
                               m_w_gate[j], m_b_gate[j], m_head_norm[j], m_skip[j], m_w_out[j])
        else:
            lb = jnp.stack([lb_fwd[i], lb_bwd[i]])
            mix = _hgrn_mixer(xs, mix_norm[i], h_w_in[j], h_b_f[j], lb, h_head_norm[j], h_w_out[j])
        ple = (p.reshape(depth, b * s, -1), i, ple_norm[i], ple_w_gate[i].astype(BF16), ple_w_proj[i].astype(BF16))
        x = _ffn(x, ffn2_norm[i], ffn2_w_in[i].astype(BF16), ffn2_w_out[i].astype(BF16), mix=mix, ple=ple,
                 final_g=final_norm if i == depth - 1 else None)
    return x.reshape(b, s, d)


def kernel(x_prompt, x_sample, p_prompt, p_sample, ffn1_norm, ffn1_w_in, ffn1_w_out, mix_norm, m_w_in, m_conv_w, m_conv_b, m_wq, m_wk, m_wv, m_w_gate, m_b_gate, m_head_norm, m_skip, m_w_out, h_w_in, h_b_f, h_lb_logits, h_head_norm, h_w_out, ffn2_norm, ffn2_w_in, ffn2_w_out, ple_norm, ple_w_gate, ple_w_proj, final_norm):
    weights = (ffn1_norm, ffn1_w_in, ffn1_w_out, mix_norm,
               m_w_in, m_conv_w, m_conv_b, m_wq, m_wk, m_wv, m_w_gate, m_b_gate, m_head_norm, m_skip, m_w_out,
               h_w_in, h_b_f, h_lb_logits, h_head_norm, h_w_out,
               ffn2_norm, ffn2_w_in, ffn2_w_out, ple_norm, ple_w_gate, ple_w_proj, final_norm)
    return (_trunk(x_prompt, p_prompt, *weights), _trunk(x_sample, p_sample, *weights))
```
